```python
import math
import jax
import jax.numpy as jnp
from jax import lax
import numpy as np

D_MODEL = 4096
BATCH = 4
SEQ = 2048
DEPTH = 2
DEC_BATCH = 8
DEC_SEQ = 1
PAST_LEN = 16384
PAGE_SIZE = 128

HEAD_DIM = 128
D_ATT = D_MODEL // 2
D_POOL = D_MODEL // 4
D_CHUNK = D_MODEL // 4
N_HEADS_ATT = D_ATT // HEAD_DIM
N_HEADS_CHUNK = D_CHUNK // HEAD_DIM
POOL_WINDOWS = (2, 4, 8, 16)
N_POOL_GROUPS = len(POOL_WINDOWS)
POOL_GROUP = D_POOL // N_POOL_GROUPS
POOL_BUF = max(POOL_WINDOWS) - 1
CHUNK = 128
QBLOCK = 128
D_FF = 4 * D_MODEL
D_IN = D_POOL + 3 * D_ATT + 2 * D_CHUNK
SPLIT_IDX = (D_POOL, D_POOL + D_ATT, D_POOL + 2 * D_ATT, D_POOL + 3 * D_ATT,
             D_POOL + 3 * D_ATT + D_CHUNK)
ALPHA = (2.0 * DEPTH) ** 0.25
BETA = (8.0 * DEPTH) ** -0.25
SB_BIAS_INIT = -8.0
LN_EPS = 1e-5

kernel_name = "hybrid_pool_stickbreak_chunkmlp_decode_step"


def layer_norm(x, g, b):
    xf = x.astype(jnp.float32)
    mu = jnp.mean(xf, axis=-1, keepdims=True)
    var = jnp.mean(jnp.square(xf - mu), axis=-1, keepdims=True)
    return ((xf - mu) * lax.rsqrt(var + LN_EPS) * g + b).astype(x.dtype)


def pool_mix(xa, buf, pos0, w_pool, pool_scale):
    B, T, _ = xa.shape
    ext_raw = jnp.concatenate([buf.astype(xa.dtype), xa], axis=1)
    ext = ext_raw.astype(jnp.float32)
    cs = jnp.pad(jnp.cumsum(ext, axis=1), ((0, 0), (1, 0), (0, 0)))
    hi = cs[:, POOL_BUF + 1:]
    pos = pos0 + jnp.arange(T)
    means = []
    for g, w in enumerate(POOL_WINDOWS):
        sl = slice(g * POOL_GROUP, (g + 1) * POOL_GROUP)
        lo = cs[:, POOL_BUF + 1 - w: POOL_BUF + 1 - w + T, sl]
        cnt = jnp.minimum(w, pos + 1).astype(jnp.float32)[None, :, None]
        means.append((hi[..., sl] - lo) / cnt)
    diff = (jnp.concatenate(means, axis=-1) - xa.astype(jnp.float32))
    diff = diff.reshape(B, T, N_POOL_GROUPS, POOL_GROUP)
    y = jnp.einsum('btgc,gce->btge', diff, w_pool.astype(jnp.float32)).reshape(B, T, D_POOL)
    y = y * pool_scale
    return y.astype(xa.dtype), ext_raw[:, -POOL_BUF:]


def stick_breaking(q, k, v, sb_bias, q_offset):
    B, T, H, d = q.shape
    S = k.shape[1]
    qb = math.gcd(T, QBLOCK)
    nb = T // qb
    scale = 1.0 / math.sqrt(d)
    kf = k.astype(jnp.float32)
    vf = v.astype(jnp.float32)
    bias = sb_bias.astype(jnp.float32)[None, :, None, None]
    kpos = jnp.arange(S)
    qblocks = jnp.moveaxis(q.reshape(B, nb, qb, H, d), 1, 0)

    def block(args):
        i, qblk = args
        qpos = q_offset + i * qb + jnp.arange(qb)
        z = jnp.einsum('bqhd,bshd->bhqs', qblk.astype(jnp.float32), kf) * scale + bias
        mask = kpos[None, :] < qpos[:, None]
        log_one_minus = jnp.where(mask, jax.nn.log_sigmoid(-z), 0.0)
        later = lax.cumsum(log_one_minus, axis=3, reverse=True) - log_one_minus
        w = jnp.where(mask, jnp.exp(jax.nn.log_sigmoid(z) + later), 0.0)
        return jnp.einsum('bhqs,bshd->bqhd', w, vf)

    out = lax.map(block, (jnp.arange(nb), qblocks))
    return jnp.moveaxis(out, 0, 1).reshape(B, T, H, d).astype(q.dtype)


def chunk_mix(u, gv, w_s, b_s):
    B, T, HC, d = gv.shape
    pad = (-T) % CHUNK
    nc = (T + pad) // CHUNK
    vp = jnp.pad(gv, ((0, 0), (0, pad), (0, 0), (0, 0))).reshape(B, nc, CHUNK, HC, d)
    tril = jnp.tril(jnp.ones((CHUNK, CHUNK), dtype=bool))
    w = jnp.where(tril[None], w_s, 0.0)
    mixed = jnp.einsum('hpq,bnqhc->bnphc', w, vp) + b_s.T[:, :, None]
    mixed = mixed.reshape(B, nc * CHUNK, HC, d)[:, :T]
    return u * mixed


def trunk_layer(x, c, pos0, pool_buf, kv_past, w_ada, b_ada, w_in, sb_bias, w_pool, pool_scale,
                ln_v_g, ln_v_b, w_s, b_s, w_out, ln1_g, ln1_b, w_ff1, w_ff2, ln2_g, ln2_b):
    B, T, _ = x.shape
    mod = jax.nn.silu(c) @ w_ada + b_ada
    sh1, sc1, g1, sh2, sc2, g2 = [m[:, None, :] for m in jnp.split(mod, 6, axis=-1)]

    h = x * (1 + sc1) + sh1
    p = h @ w_in
    xa, q, k, v, u, gv = jnp.split(p, SPLIT_IDX, axis=-1)

    a_out, new_pool = pool_mix(xa, pool_buf, pos0, w_pool, pool_scale)

    q = q.reshape(B, T, N_HEADS_ATT, HEAD_DIM)
    k = k.reshape(B, T, N_HEADS_ATT, HEAD_DIM)
    v = v.reshape(B, T, N_HEADS_ATT, HEAD_DIM)
    if kv_past is None:
        k_all, v_all = k, v
    else:
        k_all = jnp.concatenate([kv_past[0].astype(k.dtype), k], axis=1)
        v_all = jnp.concatenate([kv_past[1].astype(v.dtype), v], axis=1)
    b_out = stick_breaking(q, k_all, v_all, sb_bias, pos0).reshape(B, T, D_ATT)

    u = jax.nn.gelu(u).reshape(B, T, N_HEADS_CHUNK, HEAD_DIM)
    gv = layer_norm(jax.nn.gelu(gv).reshape(B, T, N_HEADS_CHUNK, HEAD_DIM),
                    ln_v_g.reshape(N_HEADS_CHUNK, HEAD_DIM), ln_v_b.reshape(N_HEADS_CHUNK, HEAD_DIM))
    c_out = chunk_mix(u, gv, w_s, b_s).reshape(B, T, D_CHUNK)

    mix = jnp.concatenate([a_out, b_out, c_out], axis=-1) @ w_out
    x = layer_norm(ALPHA * x + g1 * mix, ln1_g, ln1_b)

    h2 = x * (1 + sc2) + sh2
    f = jnp.square(jax.nn.relu(h2 @ w_ff1)) @ w_ff2
    x = layer_norm(ALPHA * x + g2 * f, ln2_g, ln2_b)
    return x, new_pool, k, v, gv


def setup_inputs(seed: int = 0) -> dict:
    key = jax.random.key(seed)
    ks = jax.random.split(key, 32)
    f32 = jnp.float32
    n_pages = PAST_LEN // PAGE_SIZE
    n_used = DEC_BATCH * n_pages
    n_pool_pages = (5 * n_used) // 4

    def nrm(k, shape, s=1.0):
        return jax.random.normal(k, shape, f32) * s

    perm = jax.random.permutation(ks[5], n_pool_pages)[:n_used]
    page_table = perm.reshape(DEC_BATCH, n_pages).astype(jnp.int32)
    cache_shape = (DEPTH, n_pool_pages, PAGE_SIZE, N_HEADS_ATT, HEAD_DIM)
    return {
        "x_prompt": nrm(ks[0], (BATCH, SEQ, D_MODEL)),
        "x_sample": nrm(ks[1], (DEC_BATCH, DEC_SEQ, D_MODEL)),
        "cache_k": nrm(ks[2], cache_shape),
        "cache_v": nrm(ks[3], cache_shape),
        "state_pool": nrm(ks[4], (DEPTH, DEC_BATCH, POOL_BUF, D_POOL)),
        "page_table": page_table,
        "c_prompt": nrm(ks[6], (BATCH, D_MODEL)),
        "c_sample": nrm(ks[7], (DEC_BATCH, D_MODEL)),
        "ln0_g": 1.0 + nrm(ks[8], (D_MODEL,), 0.02),
        "ln0_b": nrm(ks[9], (D_MODEL,), 0.02),
        "w_ada": nrm(ks[10], (DEPTH, D_MODEL, 6 * D_MODEL), D_MODEL ** -0.5),
        "b_ada": nrm(ks[11], (DEPTH, 6 * D_MODEL), 0.02),
        "w_in": nrm(ks[12], (DEPTH, D_MODEL, D_IN), D_MODEL ** -0.5),
        "sb_bias": SB_BIAS_INIT + nrm(ks[26], (DEPTH, N_HEADS_ATT), 0.1),
        "w_pool": nrm(ks[13], (DEPTH, N_POOL_GROUPS, POOL_GROUP, POOL_GROUP), POOL_GROUP ** -0.5),
        "pool_scale": 1.0 + nrm(ks[14], (DEPTH, D_POOL), 0.02),
        "ln_v_g": 1.0 + nrm(ks[15], (DEPTH, D_CHUNK), 0.02),
        "ln_v_b": nrm(ks[16], (DEPTH, D_CHUNK), 0.02),
        "w_s": nrm(ks[17], (DEPTH, N_HEADS_CHUNK, CHUNK, CHUNK), CHUNK ** -0.5),
        "b_s": 1.0 + nrm(ks[18], (DEPTH, N_HEADS_CHUNK, CHUNK), 0.02),
        "w_out": nrm(ks[19], (DEPTH, D_MODEL, D_MODEL), BETA * D_MODEL ** -0.5),
        "ln1_g": 1.0 + nrm(ks[20], (DEPTH, D_MODEL), 0.02),
        "ln1_b": nrm(ks[21], (DEPTH, D_MODEL), 0.02),
        "w_ff1": nrm(ks[22], (DEPTH, D_MODEL, D_FF), D_MODEL ** -0.5),
        "w_ff2": nrm(ks[23], (DEPTH, D_FF, D_MODEL), BETA * D_FF ** -0.5),
        "ln2_g": 1.0 + nrm(ks[24], (DEPTH, D_MODEL), 0.02),
        "ln2_b": nrm(ks[25], (DEPTH, D_MODEL), 0.02),
    }


def reference(x_prompt, x_sample, cache_k, cache_v, state_pool, page_table, c_prompt, c_sample,
              ln0_g, ln0_b, w_ada, b_ada, w_in, sb_bias, w_pool, pool_scale, ln_v_g, ln_v_b, w_s, b_s,
              w_out, ln1_g, ln1_b, w_ff1, w_ff2, ln2_g, ln2_b):
    n_seq = page_table.shape[0]
    yp = layer_norm(x_prompt, ln0_g, ln0_b)
    ys = layer_norm(x_sample, ln0_g, ln0_b)
    zero_buf = jnp.zeros((x_prompt.shape[0], POOL_BUF, D_POOL), x_prompt.dtype)
    kp_l, vp_l, pp_l, cp_l = [], [], [], []
    ks_l, vs_l, ps_l, cs_l = [], [], [], []
    for l in range(DEPTH):
        w = (w_ada[l], b_ada[l], w_in[l], sb_bias[l], w_pool[l], pool_scale[l], ln_v_g[l], ln_v_b[l],
             w_s[l], b_s[l], w_out[l], ln1_g[l], ln1_b[l], w_ff1[l], w_ff2[l], ln2_g[l], ln2_b[l])
        yp, pool_p, k_p, v_p, cv_p = trunk_layer(yp, c_prompt, 0, zero_buf, None, *w)
        k_past = cache_k[l][page_table].reshape(n_seq, PAST_LEN, N_HEADS_ATT, HEAD_DIM)
        v_past = cache_v[l][page_table].reshape(n_seq, PAST_LEN, N_HEADS_ATT, HEAD_DIM)
        ys, pool_s, k_s, v_s, cv_s = trunk_layer(ys, c_sample, PAST_LEN, state_pool[l],
                                                 (k_past, v_past), *w)
        kp_l.append(k_p); vp_l.append(v_p); pp_l.append(pool_p); cp_l.append(cv_p)
        ks_l.append(k_s); vs_l.append(v_s); ps_l.append(pool_s); cs_l.append(cv_s)
    k_prompt = jnp.stack(kp_l)
    v_prompt = jnp.stack(vp_l)
    pool_prompt = jnp.stack(pp_l)
    chunkv_prompt = jnp.stack(cp_l)
    k_sample = jnp.stack(ks_l)
    v_sample = jnp.stack(vs_l)
    pool_sample = jnp.stack(ps_l)
    chunkv_sample = jnp.stack(cs_l)
    return (yp, ys, k_prompt, v_prompt, pool_prompt, chunkv_prompt,
            k_sample, v_sample, pool_sample, chunkv_sample)
```

```python
import functools
import math

import jax
import jax.numpy as jnp
from jax import lax
from jax.experimental import pallas as pl
from jax.experimental.pallas import tpu as pltpu

F32 = jnp.float32
BF16 = jnp.bfloat16

D_MODEL = 4096
HEAD_DIM = 128
D_ATT = D_MODEL // 2
D_POOL = D_MODEL // 4
D_CHUNK = D_MODEL // 4
N_HEADS_ATT = D_ATT // HEAD_DIM
N_HEADS_CHUNK = D_CHUNK // HEAD_DIM
POOL_WINDOWS = (2, 4, 8, 16)
POOL_GROUP = D_POOL // len(POOL_WINDOWS)
POOL_BUF = max(POOL_WINDOWS) - 1
POOL_HALO = 16
CHUNK = 128
D_FF = 4 * D_MODEL
D_IN = D_POOL + 3 * D_ATT + 2 * D_CHUNK
DEPTH = 2
ALPHA = (2.0 * DEPTH) ** 0.25
LN_EPS = 1e-5
ATT_SCALE = 1.0 / math.sqrt(HEAD_DIM)

Q_COL128 = D_POOL // HEAD_DIM
K_COL128 = (D_POOL + D_ATT) // HEAD_DIM
V_COL128 = (D_POOL + 2 * D_ATT) // HEAD_DIM
U_COL1024 = (D_POOL + 3 * D_ATT) // D_CHUNK
GV_COL1024 = U_COL1024 + 1

VMEM_LIMIT_BYTES = 56 * 1024 * 1024


def _params(*sem):
    return pltpu.CompilerParams(dimension_semantics=sem, vmem_limit_bytes=VMEM_LIMIT_BYTES)


def _gelu_tanh(x):
    c = math.sqrt(2.0 / math.pi)
    return 0.5 * x * (1.0 + jnp.tanh(c * (x + 0.044715 * (x * x * x))))


def _softplus(z):
    return jnp.maximum(z, 0.0) + jnp.log(1.0 + jnp.exp(-jnp.abs(z)))


def _layer_norm_rows(x, g, b):
    mu = jnp.mean(x, axis=-1, keepdims=True)
    xc = x - mu
    var = jnp.mean(xc * xc, axis=-1, keepdims=True)
    return xc * lax.rsqrt(var + LN_EPS) * g + b


def _ada_kernel(c_ref, w_ref, b_ref, o_ref):
    c = c_ref[...]
    s = (c * jax.nn.sigmoid(c)).astype(BF16)
    o_ref[...] = jnp.dot(s, w_ref[...].astype(BF16), preferred_element_type=F32) + b_ref[...]


def _ada(cs, w_ada, b_ada, tn=512):
    rows = cs.shape[0]
    n = w_ada.shape[-1]
    return pl.pallas_call(
        _ada_kernel,
        grid=(DEPTH, n // tn),
        in_specs=[
            pl.BlockSpec((rows, D_MODEL), lambda l, j: (0, 0)),
            pl.BlockSpec((None, D_MODEL, tn), lambda l, j: (l, 0, j)),
            pl.BlockSpec((None, 1, tn), lambda l, j: (l, 0, j)),
        ],
        out_specs=pl.BlockSpec((None, rows, tn), lambda l, j: (l, 0, j)),
        out_shape=jax.ShapeDtypeStruct((DEPTH, rows, n), F32),
        compiler_params=_params("arbitrary", "arbitrary"),
        name="ada_mod",
    )(cs, w_ada, b_ada.reshape(DEPTH, 1, n))


def _ln_mod_kernel(*refs, has_res, has_mod):
    it = iter(refs)
    x_ref = next(it)
    if has_res:
        y_ref, gate_ref = next(it), next(it)
    g_ref, b_ref = next(it), next(it)
    if has_mod:
        sc_ref, sh_ref = next(it), next(it)
    xo_ref = next(it)
    x = x_ref[...]
    if has_res:
        x = ALPHA * x + gate_ref[...] * y_ref[...].astype(F32)
    xn = _layer_norm_rows(x, g_ref[...], b_ref[...])
    xo_ref[...] = xn
    if has_mod:
        ho_ref = next(it)
        ho_ref[...] = (xn * (1.0 + sc_ref[...]) + sh_ref[...]).astype(BF16)


def _ln_mod(x, ln_g, ln_b, *, y=None, mod=None, gate_col=None, mod_res=None, sc_col=None, tt=256):
    B, T, D = x.shape
    tt = min(tt, T)
    has_res = y is not None
    has_mod = mod is not None
    row = pl.BlockSpec((None, tt, D), lambda b, i: (b, i, 0))
    vec = pl.BlockSpec((1, D), lambda b, i: (0, 0))

    def modspec(m, col):
        tm = m.shape[1]
        return pl.BlockSpec((None, tm, D), lambda b, i: (b, 0, col))

    args, specs = [x], [row]
    if has_res:
        args += [y, mod_res]
        specs += [row, modspec(mod_res, gate_col)]
    args += [ln_g.reshape(1, D), ln_b.reshape(1, D)]
    specs += [vec, vec]
    if has_mod:
        args += [mod, mod]
        specs += [modspec(mod, sc_col), modspec(mod, sc_col - 1)]
    out_shape = [jax.ShapeDtypeStruct((B, T, D), F32)]
    out_specs = [row]
    if has_mod:
        out_shape.append(jax.ShapeDtypeStruct((B, T, D), BF16))
        out_specs.append(row)
    res = pl.pallas_call(
        functools.partial(_ln_mod_kernel, has_res=has_res, has_mod=has_mod),
        grid=(B, T // tt),
        in_specs=specs,
        out_specs=out_specs,
        out_shape=out_shape,
        compiler_params=_params("parallel", "parallel"),
        name="ln_mod",
    )(*args)
    return (res[0], res[1]) if has_mod else (res[0], None)


def _mm_kernel(x_ref, w_ref, o_ref, *scratch, nk, relu2):
    def finish(acc):
        if relu2:
            r = jnp.maximum(acc, 0.0)
            acc = r * r
        o_ref[...] = acc.astype(o_ref.dtype)

    part = jnp.dot(x_ref[...].astype(BF16), w_ref[...], preferred_element_type=F32)
    if nk == 1:
        finish(part)
        return
    acc_ref, = scratch
    k = pl.program_id(2)

    @pl.when(k == 0)
    def _():
        acc_ref[...] = part

    @pl.when(k > 0)
    def _():
        acc_ref[...] += part

    @pl.when(k == nk - 1)
    def _():
        finish(acc_ref[...])


def _matmul(x, w, *, tm, tn, tk, out_dtype=F32, relu2=False):
    M, K = x.shape
    N = w.shape[1]
    tm, tn, tk = min(tm, M), min(tn, N), min(tk, K)
    nk = K // tk
    scratch = [pltpu.VMEM((tm, tn), F32)] if nk > 1 else []
    return pl.pallas_call(
        functools.partial(_mm_kernel, nk=nk, relu2=relu2),
        grid=(M // tm, N // tn, nk),
        in_specs=[
            pl.BlockSpec((tm, tk), lambda i, j, k: (i, k)),
            pl.BlockSpec((tk, tn), lambda i, j, k: (k, j)),
        ],
        out_specs=pl.BlockSpec((tm, tn), lambda i, j, k: (i, j)),
        out_shape=jax.ShapeDtypeStruct((M, N), out_dtype),
        scratch_shapes=scratch,
        compiler_params=_params("parallel", "parallel", "arbitrary"),
        name="dense_matmul",
    )(x, w)


def _pool_prompt_kernel(xa_ref, halo_ref, w_ref, scale_ref, o_ref, ext_ref, *, tt):
    i = pl.program_id(1)
    xa = xa_ref[...]
    ext_ref[pl.ds(POOL_HALO, tt), :] = xa

    @pl.when(i == 0)
    def _():
        ext_ref[pl.ds(0, POOL_HALO), :] = jnp.zeros((POOL_HALO, D_POOL), F32)

    @pl.when(i > 0)
    def _():
        ext_ref[pl.ds(0, POOL_HALO), :] = halo_ref[...]

    pos = i * tt + lax.broadcasted_iota(jnp.int32, (tt, 1), 0)
    outs = []
    for g, win in enumerate(POOL_WINDOWS):
        cols = pl.ds(g * POOL_GROUP, POOL_GROUP)
        acc = ext_ref[pl.ds(POOL_HALO, tt), cols]
        for back in range(1, win):
            acc = acc + ext_ref[pl.ds(POOL_HALO - back, tt), cols]
        cnt = jnp.minimum(win, pos + 1).astype(F32)
        diff = acc / cnt - xa[:, g * POOL_GROUP:(g + 1) * POOL_GROUP]
        outs.append(jnp.dot(diff.astype(BF16), w_ref[g].astype(BF16), preferred_element_type=F32))
    y = jnp.concatenate(outs, axis=-1) * scale_ref[...]
    o_ref[...] = y.astype(o_ref.dtype)


def _pool_prompt(p, w_pool, pool_scale, tt=256):
    B, T, _ = p.shape
    halo_blocks = tt // POOL_HALO
    return pl.pallas_call(
        functools.partial(_pool_prompt_kernel, tt=tt),
        grid=(B, T // tt),
        in_specs=[
            pl.BlockSpec((None, tt, D_POOL), lambda b, i: (b, i, 0)),
            pl.BlockSpec((None, POOL_HALO, D_POOL),
                         lambda b, i: (b, jnp.maximum(i * halo_blocks - 1, 0), 0)),
            pl.BlockSpec((len(POOL_WINDOWS), POOL_GROUP, POOL_GROUP), lambda b, i: (0, 0, 0)),
            pl.BlockSpec((1, D_POOL), lambda b, i: (0, 0)),
        ],
        out_specs=pl.BlockSpec((None, tt, D_POOL), lambda b, i: (b, i, 0)),
        out_shape=jax.ShapeDtypeStruct((B, T, D_POOL), BF16),
        scratch_shapes=[pltpu.VMEM((POOL_HALO + tt, D_POOL), F32)],
        compiler_params=_params("parallel", "arbitrary"),
        name="pool_prompt",
    )(p, p, w_pool, pool_scale.reshape(1, D_POOL))


def _chunk_prompt_kernel(u_ref, gv_ref, g_ref, b_ref, ws_ref, bs_ref, cv_ref, o_ref):
    row = lax.broadcasted_iota(jnp.int32, (CHUNK, CHUNK), 0)
    col = lax.broadcasted_iota(jnp.int32, (CHUNK, CHUNK), 1)
    tril = col <= row
    for h in range(N_HEADS_CHUNK):
        cols = pl.ds(h * HEAD_DIM, HEAD_DIM)
        gvn = _layer_norm_rows(_gelu_tanh(gv_ref[:, cols]), g_ref[:, cols], b_ref[:, cols])
        cv_ref[:, cols] = gvn
        w = jnp.where(tril, ws_ref[h], 0.0).astype(BF16)
        mixed = jnp.dot(w, gvn.astype(BF16), preferred_element_type=F32) + bs_ref[:, h:h + 1]
        o_ref[:, cols] = (_gelu_tanh(u_ref[:, cols]) * mixed).astype(o_ref.dtype)


def _chunk_prompt(p, ln_v_g, ln_v_b, w_s, b_s):
    B, T, _ = p.shape
    vec = pl.BlockSpec((1, D_CHUNK), lambda b, i: (0, 0))
    return pl.pallas_call(
        _chunk_prompt_kernel,
        grid=(B, T // CHUNK),
        in_specs=[
            pl.BlockSpec((None, CHUNK, D_CHUNK), lambda b, i: (b, i, U_COL1024)),
            pl.BlockSpec((None, CHUNK, D_CHUNK), lambda b, i: (b, i, GV_COL1024)),
            vec, vec,
            pl.BlockSpec((N_HEADS_CHUNK, CHUNK, CHUNK), lambda b, i: (0, 0, 0)),
            pl.BlockSpec((CHUNK, N_HEADS_CHUNK), lambda b, i: (0, 0)),
        ],
        out_specs=[
            pl.BlockSpec((None, CHUNK, D_CHUNK), lambda b, i: (b, i, 0)),
            pl.BlockSpec((None, CHUNK, D_CHUNK), lambda b, i: (b, i, 0)),
        ],
        out_shape=[
            jax.ShapeDtypeStruct((B, T, D_CHUNK), F32),
            jax.ShapeDtypeStruct((B, T, D_CHUNK), BF16),
        ],
        compiler_params=_params("parallel", "parallel"),
        name="chunk_prompt",
    )(p, p, ln_v_g.reshape(1, D_CHUNK), ln_v_b.reshape(1, D_CHUNK), w_s, b_s.T)


def _sb_block(q, kb, vb, bias, upper, carry, acc, mask=None):
    z = lax.dot_general(q, kb, (((1,), (1,)), ((), ())), preferred_element_type=F32) + bias
    sp = _softplus(z)
    if mask is not None:
        sp = jnp.where(mask, sp, 0.0)
    later = jnp.dot(sp.astype(BF16), upper, preferred_element_type=F32)
    w = jnp.exp(z - sp - later - carry)
    if mask is not None:
        w = jnp.where(mask, w, 0.0)
    acc = acc + jnp.dot(w.astype(BF16), vb, preferred_element_type=F32)
    carry = carry + later[:, 0:1] + sp[:, 0:1]
    return carry, acc


def _sb_prompt_kernel(bias_ref, q_ref, k_ref, v_ref, o_ref, kb_ref, vb_ref, *, tq):
    h = pl.program_id(1)
    i = pl.program_id(2)

    @pl.when(i == 0)
    def _():
        kb_ref[...] = k_ref[...].astype(BF16)
        vb_ref[...] = v_ref[...].astype(BF16)

    bias = bias_ref[h]
    q = (q_ref[...] * ATT_SCALE).astype(BF16)
    row = lax.broadcasted_iota(jnp.int32, (tq, tq), 0)
    col = lax.broadcasted_iota(jnp.int32, (tq, tq), 1)
    upper = (row > col).astype(BF16)
    carry = jnp.zeros((tq, 1), F32)
    acc = jnp.zeros((tq, HEAD_DIM), F32)
    rows_i = pl.ds(pl.multiple_of(i * tq, tq), tq)
    carry, acc = _sb_block(q, kb_ref[rows_i, :], vb_ref[rows_i, :], bias, upper, carry, acc,
                           mask=col < row)

    def body(n, state):
        j = i - 1 - n
        rows_j = pl.ds(pl.multiple_of(j * tq, tq), tq)
        return _sb_block(q, kb_ref[rows_j, :], vb_ref[rows_j, :], bias, upper, *state)

    carry, acc = lax.fori_loop(0, i, body, (carry, acc))
    o_ref[...] = acc.astype(o_ref.dtype)


def _sb_prompt(p, sb_bias, tq=256):
    B, T, _ = p.shape
    return pl.pallas_call(
        functools.partial(_sb_prompt_kernel, tq=tq),
        grid=(B, N_HEADS_ATT, T // tq),
        in_specs=[
            pl.BlockSpec(memory_space=pltpu.SMEM),
            pl.BlockSpec((None, tq, HEAD_DIM), lambda b, h, i: (b, i, Q_COL128 + h)),
            pl.BlockSpec((None, T, HEAD_DIM), lambda b, h, i: (b, 0, K_COL128 + h)),
            pl.BlockSpec((None, T, HEAD_DIM), lambda b, h, i: (b, 0, V_COL128 + h)),
        ],
        out_specs=pl.BlockSpec((None, tq, HEAD_DIM), lambda b, h, i: (b, i, h)),
        out_shape=jax.ShapeDtypeStruct((B, T, D_ATT), BF16),
        scratch_shapes=[pltpu.VMEM((T, HEAD_DIM), BF16), pltpu.VMEM((T, HEAD_DIM), BF16)],
        compiler_params=_params("parallel", "parallel", "arbitrary"),
        name="sb_prompt",
    )(sb_bias, p, p, p)


def _sb_sample_kernel(pt_ref, bias_ref, q_ref, k_ref, v_ref, o_ref, qb_ref, carry_ref, acc_ref,
                      *, n_pages, page):
    del pt_ref
    j = pl.program_id(1)
    head_of_lane = lax.broadcasted_iota(jnp.int32, (N_HEADS_ATT, D_ATT), 1) // HEAD_DIM
    head_of_row = lax.broadcasted_iota(jnp.int32, (N_HEADS_ATT, D_ATT), 0)
    diag = head_of_lane == head_of_row

    @pl.when(j == 0)
    def _():
        qb_ref[...] = jnp.where(diag, q_ref[...] * ATT_SCALE, 0.0).astype(BF16)
        carry_ref[...] = jnp.zeros_like(carry_ref)
        acc_ref[...] = jnp.zeros_like(acc_ref)

    kb = k_ref[...].astype(BF16)
    vb = v_ref[...].astype(BF16)
    z = lax.dot_general(qb_ref[...], kb, (((1,), (1,)), ((), ())),
                        preferred_element_type=F32) + bias_ref[...]
    sp = _softplus(z)
    row = lax.broadcasted_iota(jnp.int32, (page, page), 0)
    col = lax.broadcasted_iota(jnp.int32, (page, page), 1)
    upper = (row > col).astype(BF16)
    later = jnp.dot(sp.astype(BF16), upper, preferred_element_type=F32)
    w = jnp.exp(z - sp - later - carry_ref[...])
    acc_ref[...] += jnp.dot(w.astype(BF16), vb, preferred_element_type=F32)
    carry_ref[...] += later[:, 0:1] + sp[:, 0:1]

    @pl.when(j == n_pages - 1)
    def _():
        o_ref[...] = jnp.sum(jnp.where(diag, acc_ref[...], 0.0), axis=0, keepdims=True)


def _sb_sample(q, cache_k, cache_v, page_table, sb_bias, layer):
    n_seq, n_pages = page_table.shape
    page = cache_k.shape[2]

    def cache_map(b, j, pt):
        return (layer, pt[b, n_pages - 1 - j], 0, 0)

    grid_spec = pltpu.PrefetchScalarGridSpec(
        num_scalar_prefetch=1,
        grid=(n_seq, n_pages),
        in_specs=[
            pl.BlockSpec((N_HEADS_ATT, 1), lambda b, j, pt: (0, 0)),
            pl.BlockSpec((None, 1, D_ATT), lambda b, j, pt: (b, 0, 0)),
            pl.BlockSpec((None, None, page, D_ATT), cache_map),
            pl.BlockSpec((None, None, page, D_ATT), cache_map),
        ],
        out_specs=pl.BlockSpec((None, 1, D_ATT), lambda b, j, pt: (b, 0, 0)),
        scratch_shapes=[
            pltpu.VMEM((N_HEADS_ATT, D_ATT), BF16),
            pltpu.VMEM((N_HEADS_ATT, 1), F32),
            pltpu.VMEM((N_HEADS_ATT, D_ATT), F32),
        ],
    )
    return pl.pallas_call(
        functools.partial(_sb_sample_kernel, n_pages=n_pages, page=page),
        grid_spec=grid_spec,
        out_shape=jax.ShapeDtypeStruct((n_seq, 1, D_ATT), F32),
        compiler_params=_params("parallel", "arbitrary"),
        name="sb_sample",
    )(page_table, sb_bias.reshape(N_HEADS_ATT, 1), q, cache_k, cache_v)


def _mix_sample_kernel(p_ref, att_ref, state_ref, wp_ref, scale_ref, g_ref, b_ref, w0_ref, b0_ref,
                       cv_ref, o_ref):
    xa = p_ref[:, 0:D_POOL]
    outs = []
    for g, win in enumerate(POOL_WINDOWS):
        lo, hi = g * POOL_GROUP, (g + 1) * POOL_GROUP
        acc = xa[:, lo:hi]
        for back in range(1, win):
            acc = acc + state_ref[POOL_BUF - back, :, lo:hi]
        diff = acc / float(win) - xa[:, lo:hi]
        outs.append(jnp.dot(diff.astype(BF16), wp_ref[g].astype(BF16), preferred_element_type=F32))
    o_ref[:, 0:D_POOL] = (jnp.concatenate(outs, axis=-1) * scale_ref[...]).astype(o_ref.dtype)
    o_ref[:, D_POOL:D_POOL + D_ATT] = att_ref[...].astype(o_ref.dtype)
    u0 = D_POOL + 3 * D_ATT
    for h in range(N_HEADS_CHUNK):
        lo, hi = h * HEAD_DIM, (h + 1) * HEAD_DIM
        gvn = _layer_norm_rows(_gelu_tanh(p_ref[:, u0 + D_CHUNK + lo:u0 + D_CHUNK + hi]),
                               g_ref[:, lo:hi], b_ref[:, lo:hi])
        cv_ref[:, lo:hi] = gvn
        mixed = w0_ref[:, lo:hi] * gvn + b0_ref[:, lo:hi]
        o_ref[:, D_POOL + D_ATT + lo:D_POOL + D_ATT + hi] = (
            _gelu_tanh(p_ref[:, u0 + lo:u0 + hi]) * mixed).astype(o_ref.dtype)


def _mix_sample(p, att, state, w_pool, pool_scale, ln_v_g, ln_v_b, w_s, b_s):
    S = p.shape[0]
    w0 = jnp.repeat(w_s[:, 0, 0], HEAD_DIM).reshape(1, D_CHUNK)
    b0 = jnp.repeat(b_s[:, 0], HEAD_DIM).reshape(1, D_CHUNK)
    return pl.pallas_call(
        _mix_sample_kernel,
        out_shape=[
            jax.ShapeDtypeStruct((S, D_CHUNK), F32),
            jax.ShapeDtypeStruct((S, D_MODEL), F32),
        ],
        compiler_params=pltpu.CompilerParams(vmem_limit_bytes=VMEM_LIMIT_BYTES),
        name="mix_sample",
    )(p, att, jnp.swapaxes(state, 0, 1), w_pool, pool_scale.reshape(1, D_POOL), ln_v_g.reshape(1, D_CHUNK),
      ln_v_b.reshape(1, D_CHUNK), w0, b0)


def kernel(x_prompt, x_sample, cache_k, cache_v, state_pool, page_table, c_prompt, c_sample,
           ln0_g, ln0_b, w_ada, b_ada, w_in, sb_bias, w_pool, pool_scale, ln_v_g, ln_v_b, w_s, b_s,
           w_out, ln1_g, ln1_b, w_ff1, w_ff2, ln2_g, ln2_b):
    B, T, D = x_prompt.shape
    S = x_sample.shape[0]
    n_pool_pages, page = cache_k.shape[1], cache_k.shape[2]

    w_in_b, w_out_b = w_in.astype(BF16), w_out.astype(BF16)
    w_ff1_b, w_ff2_b = w_ff1.astype(BF16), w_ff2.astype(BF16)
    cache_k2 = cache_k.reshape(DEPTH, n_pool_pages, page, D_ATT)
    cache_v2 = cache_v.reshape(DEPTH, n_pool_pages, page, D_ATT)

    pad = (-(B + S)) % 8
    cs = jnp.concatenate([c_prompt, c_sample, jnp.zeros((pad, D), F32)], axis=0)
    mod = _ada(cs, w_ada, b_ada)
    mod_p = [mod[l, :B].reshape(B, 1, 6 * D) for l in range(DEPTH)]
    mod_s = [mod[l, B:B + S].reshape(1, S, 6 * D) for l in range(DEPTH)]
    xs = x_sample.reshape(1, S, D)

    xp, hp = _ln_mod(x_prompt, ln0_g, ln0_b, mod=mod_p[0], sc_col=1)
    xs, hs = _ln_mod(xs, ln0_g, ln0_b, mod=mod_s[0], sc_col=1)

    outs = {k: [] for k in ("kp", "vp", "pp", "cp", "ks", "vs", "ps", "cs")}
    for l in range(DEPTH):
        p = _matmul(hp.reshape(B * T, D), w_in_b[l], tm=1024, tn=1024, tk=D).reshape(B, T, D_IN)
        a_out = _pool_prompt(p, w_pool[l], pool_scale[l])
        b_out = _sb_prompt(p, sb_bias[l])
        cv_p, c_out = _chunk_prompt(p, ln_v_g[l], ln_v_b[l], w_s[l], b_s[l])
        mix = jnp.concatenate([a_out, b_out, c_out], axis=-1).reshape(B * T, D)
        mo = _matmul(mix, w_out_b[l], tm=1024, tn=1024, tk=D).reshape(B, T, D)
        xp, h2p = _ln_mod(xp, ln1_g[l], ln1_b[l], y=mo, mod_res=mod_p[l], gate_col=2,
                          mod=mod_p[l], sc_col=4)
        outs["kp"].append(p[:, :, D_POOL + D_ATT:D_POOL + 2 * D_ATT].reshape(B, T, N_HEADS_ATT, HEAD_DIM))
        outs["vp"].append(p[:, :, D_POOL + 2 * D_ATT:D_POOL + 3 * D_ATT].reshape(B, T, N_HEADS_ATT, HEAD_DIM))
        outs["pp"].append(p[:, T - POOL_BUF:, :D_POOL])
        outs["cp"].append(cv_p.reshape(B, T, N_HEADS_CHUNK, HEAD_DIM))

        ps = _matmul(hs.reshape(S, D), w_in_b[l], tm=S, tn=1024, tk=D)
        q_s = ps[:, D_POOL:D_POOL + D_ATT].reshape(S, 1, D_ATT)
        att = _sb_sample(q_s, cache_k2, cache_v2, page_table, sb_bias[l], l).reshape(S, D_ATT)
        cv_s, mix_s = _mix_sample(ps, att, state_pool[l], w_pool[l], pool_scale[l], ln_v_g[l],
                                  ln_v_b[l], w_s[l], b_s[l])
        mo_s = _matmul(mix_s, w_out_b[l], tm=S, tn=1024, tk=D).reshape(1, S, D)
        xs, h2s = _ln_mod(xs, ln1_g[l], ln1_b[l], y=mo_s, mod_res=mod_s[l], gate_col=2,
                          mod=mod_s[l], sc_col=4)
        outs["ks"].append(ps[:, D_POOL + D_ATT:D_POOL + 2 * D_ATT].reshape(S, 1, N_HEADS_ATT, HEAD_DIM))
        outs["vs"].append(ps[:, D_POOL + 2 * D_ATT:D_POOL + 3 * D_ATT].reshape(S, 1, N_HEADS_ATT, HEAD_DIM))
        outs["ps"].append(jnp.concatenate([state_pool[l][:, 1:], ps[:, None, :D_POOL]], axis=1))
        outs["cs"].append(cv_s.reshape(S, 1, N_HEADS_CHUNK, HEAD_DIM))

        nxt = l + 1 < DEPTH
        a = _matmul(h2p.reshape(B * T, D), w_ff1_b[l], tm=1024, tn=1024, tk=D, out_dtype=BF16, relu2=True)
        f = _matmul(a, w_ff2_b[l], tm=1024, tn=1024, tk=2048).reshape(B, T, D)
        xp, hp = _ln_mod(xp, ln2_g[l], ln2_b[l], y=f, mod_res=mod_p[l], gate_col=5,
                         mod=mod_p[l + 1] if nxt else None, sc_col=1)
        a_s = _matmul(h2s.reshape(S, D), w_ff1_b[l], tm=S, tn=1024, tk=D, relu2=True)
        f_s = _matmul(a_s, w_ff2_b[l], tm=S, tn=1024, tk=D).reshape(1, S, D)
        xs, hs = _ln_mod(xs, ln2_g[l], ln2_b[l], y=f_s, mod_res=mod_s[l], gate_col=5,
                         mod=mod_s[l + 1] if nxt else None, sc_col=1)

    st = {k: jnp.stack(v) for k, v in outs.items()}
    return (xp, xs.reshape(S, 1, D), st["kp"], st["vp"], st["pp"], st["cp"],
            st["ks"], st["vs"], st["ps"], st["cs"])
```

```python
import functools
import math

import jax
import jax.numpy as jnp
from jax import lax
from jax.experimental import pallas as pl
from jax.experimental.pallas import tpu as pltpu

F32 = jnp.float32
BF16 = jnp.bfloat16

D_MODEL = 4096
HEAD_DIM = 128
D_ATT = D_MODEL // 2
D_POOL = D_MODEL // 4
D_CHUNK = D_MODEL // 4
N_HEADS_ATT = D_ATT // HEAD_DIM
N_HEADS_CHUNK = D_CHUNK // HEAD_DIM
POOL_WINDOWS = (2, 4, 8, 16)
POOL_GROUP = D_POOL // len(POOL_WINDOWS)
POOL_BUF = max(POOL_WINDOWS) - 1
POOL_HALO = 16
CHUNK = 128
D_FF = 4 * D_MODEL
D_IN = D_POOL + 3 * D_ATT + 2 * D_CHUNK
DEPTH = 2
ALPHA = (2.0 * DEPTH) ** 0.25
LN_EPS = 1e-5
ATT_SCALE = 1.0 / math.sqrt(HEAD_DIM)

Q_COL128 = D_POOL // HEAD_DIM
K_COL128 = (D_POOL + D_ATT) // HEAD_DIM
V_COL128 = (D_POOL + 2 * D_ATT) // HEAD_DIM
K_COL1024 = (D_POOL + D_ATT) // 1024
V_COL1024 = (D_POOL + 2 * D_ATT) // 1024
U_COL1024 = (D_POOL + 3 * D_ATT) // D_CHUNK
GV_COL1024 = U_COL1024 + 1

VMEM_LIMIT_BYTES = 56 * 1024 * 1024


def _params(*sem):
    return pltpu.CompilerParams(dimension_semantics=sem, vmem_limit_bytes=VMEM_LIMIT_BYTES)


def _gelu_tanh(x):
    c = math.sqrt(2.0 / math.pi)
    return 0.5 * x * (1.0 + jnp.tanh(c * (x + 0.044715 * (x * x * x))))


def _softplus(z):
    return jnp.maximum(z, 0.0) + jnp.log(1.0 + jnp.exp(-jnp.abs(z)))


def _layer_norm_rows(x, g, b):
    mu = jnp.mean(x, axis=-1, keepdims=True)
    xc = x - mu
    var = jnp.mean(xc * xc, axis=-1, keepdims=True)
    return xc * lax.rsqrt(var + LN_EPS) * g + b


def _ada_kernel(c_ref, w_ref, b_ref, o_ref):
    c = c_ref[...]
    s = (c * jax.nn.sigmoid(c)).astype(BF16)
    o_ref[...] = jnp.dot(s, w_ref[...].astype(BF16), preferred_element_type=F32) + b_ref[...]


def _ada(cs, w_ada, b_ada, tn=512):
    rows = cs.shape[0]
    n = w_ada.shape[-1]
    return pl.pallas_call(
        _ada_kernel,
        grid=(DEPTH, n // tn),
        in_specs=[
            pl.BlockSpec((rows, D_MODEL), lambda l, j: (0, 0)),
            pl.BlockSpec((None, D_MODEL, tn), lambda l, j: (l, 0, j)),
            pl.BlockSpec((None, 1, tn), lambda l, j: (l, 0, j)),
        ],
        out_specs=pl.BlockSpec((None, rows, tn), lambda l, j: (l, 0, j)),
        out_shape=jax.ShapeDtypeStruct((DEPTH, rows, n), F32),
        compiler_params=_params("arbitrary", "arbitrary"),
        name="ada_mod",
    )(cs, w_ada, b_ada.reshape(DEPTH, 1, n))


def _ln_mod_kernel(*refs, has_res, has_mod):
    it = iter(refs)
    x_ref = next(it)
    if has_res:
        y_ref, gate_ref = next(it), next(it)
    g_ref, b_ref = next(it), next(it)
    if has_mod:
        sc_ref, sh_ref = next(it), next(it)
    xo_ref = next(it)
    x = x_ref[...]
    if has_res:
        x = ALPHA * x + gate_ref[...] * y_ref[...].astype(F32)
    xn = _layer_norm_rows(x, g_ref[...], b_ref[...])
    xo_ref[...] = xn
    if has_mod:
        ho_ref = next(it)
        ho_ref[...] = (xn * (1.0 + sc_ref[...]) + sh_ref[...]).astype(ho_ref.dtype)


def _ln_mod(x, ln_g, ln_b, *, y=None, mod=None, gate_col=None, mod_res=None, sc_col=None,
            h_dtype=BF16, tt=256):
    B, T, D = x.shape
    tt = min(tt, T)
    has_res = y is not None
    has_mod = mod is not None
    row = pl.BlockSpec((None, tt, D), lambda b, i: (b, i, 0))
    vec = pl.BlockSpec((1, D), lambda b, i: (0, 0))

    def modspec(m, col):
        tm = m.shape[1]
        return pl.BlockSpec((None, tm, D), lambda b, i: (b, 0, col))

    args, specs = [x], [row]
    if has_res:
        args += [y, mod_res]
        specs += [row, modspec(mod_res, gate_col)]
    args += [ln_g.reshape(1, D), ln_b.reshape(1, D)]
    specs += [vec, vec]
    if has_mod:
        args += [mod, mod]
        specs += [modspec(mod, sc_col), modspec(mod, sc_col - 1)]
    out_shape = [jax.ShapeDtypeStruct((B, T, D), F32)]
    out_specs = [row]
    if has_mod:
        out_shape.append(jax.ShapeDtypeStruct((B, T, D), h_dtype))
        out_specs.append(row)
    res = pl.pallas_call(
        functools.partial(_ln_mod_kernel, has_res=has_res, has_mod=has_mod),
        grid=(B, T // tt),
        in_specs=specs,
        out_specs=out_specs,
        out_shape=out_shape,
        compiler_params=_params("parallel", "parallel"),
        name="ln_mod",
    )(*args)
    return (res[0], res[1]) if has_mod else (res[0], None)


def _relu2(acc):
    r = jnp.maximum(acc, 0.0)
    return r * r


def _dense_kernel(x_ref, xs_ref, w_ref, o_ref, os_ref, wb_ref, *, relu2):
    post = _relu2 if relu2 else (lambda a: a)

    @pl.when(pl.program_id(1) == 0)
    def _():
        wb_ref[...] = w_ref[...].astype(BF16)
        os_ref[...] = post(jnp.dot(xs_ref[...].astype(BF16), wb_ref[...],
                                   preferred_element_type=F32)).astype(os_ref.dtype)

    o_ref[...] = post(jnp.dot(x_ref[...], wb_ref[...],
                              preferred_element_type=F32)).astype(o_ref.dtype)


def _dense(x, xs, w, *, out_dtype=F32, relu2=False, tm=1024, tn=512):
    M, K = x.shape
    S = xs.shape[0]
    N = w.shape[1]
    return pl.pallas_call(
        functools.partial(_dense_kernel, relu2=relu2),
        grid=(N // tn, M // tm),
        in_specs=[
            pl.BlockSpec((tm, K), lambda j, i: (i, 0)),
            pl.BlockSpec((S, K), lambda j, i: (0, 0)),
            pl.BlockSpec((K, tn), lambda j, i: (0, j)),
        ],
        out_specs=[
            pl.BlockSpec((tm, tn), lambda j, i: (i, j)),
            pl.BlockSpec((S, tn), lambda j, i: (0, j)),
        ],
        out_shape=[jax.ShapeDtypeStruct((M, N), out_dtype), jax.ShapeDtypeStruct((S, N), F32)],
        scratch_shapes=[pltpu.VMEM((K, tn), BF16)],
        compiler_params=_params("arbitrary", "arbitrary"),
        name="dense",
    )(x, xs, w)


def _mm_kernel(x_ref, w_ref, o_ref, *scratch, nk, relu2):
    def finish(acc):
        o_ref[...] = (_relu2(acc) if relu2 else acc).astype(o_ref.dtype)

    part = jnp.dot(x_ref[...].astype(BF16), w_ref[...], preferred_element_type=F32)
    if nk == 1:
        finish(part)
        return
    acc_ref, = scratch
    k = pl.program_id(2)

    @pl.when(k == 0)
    def _():
        acc_ref[...] = part

    @pl.when(k > 0)
    def _():
        acc_ref[...] += part

    @pl.when(k == nk - 1)
    def _():
        finish(acc_ref[...])


def _matmul(x, w, *, tm, tn, tk, out_dtype=F32, relu2=False):
    M, K = x.shape
    N = w.shape[1]
    tm, tn, tk = min(tm, M), min(tn, N), min(tk, K)
    nk = K // tk
    scratch = [pltpu.VMEM((tm, tn), F32)] if nk > 1 else []
    return pl.pallas_call(
        functools.partial(_mm_kernel, nk=nk, relu2=relu2),
        grid=(M // tm, N // tn, nk),
        in_specs=[
            pl.BlockSpec((tm, tk), lambda i, j, k: (i, k)),
            pl.BlockSpec((tk, tn), lambda i, j, k: (k, j)),
        ],
        out_specs=pl.BlockSpec((tm, tn), lambda i, j, k: (i, j)),
        out_shape=jax.ShapeDtypeStruct((M, N), out_dtype),
        scratch_shapes=scratch,
        compiler_params=_params("parallel", "parallel", "arbitrary"),
        name="dense_matmul",
    )(x, w)


def _pool_prompt_kernel(xa_ref, halo_ref, w_ref, scale_ref, o_ref, ext_ref, *, tt):
    i = pl.program_id(1)
    xa = xa_ref[...]
    ext_ref[pl.ds(POOL_HALO, tt), :] = xa

    @pl.when(i == 0)
    def _():
        ext_ref[pl.ds(0, POOL_HALO), :] = jnp.zeros((POOL_HALO, D_POOL), F32)

    @pl.when(i > 0)
    def _():
        ext_ref[pl.ds(0, POOL_HALO), :] = halo_ref[...]

    pos = i * tt + lax.broadcasted_iota(jnp.int32, (tt, 1), 0)
    outs = []
    for g, win in enumerate(POOL_WINDOWS):
        cols = pl.ds(g * POOL_GROUP, POOL_GROUP)
        acc = ext_ref[pl.ds(POOL_HALO, tt), cols]
        for back in range(1, win):
            acc = acc + ext_ref[pl.ds(POOL_HALO - back, tt), cols]
        cnt = jnp.minimum(win, pos + 1).astype(F32)
        diff = acc / cnt - xa[:, g * POOL_GROUP:(g + 1) * POOL_GROUP]
        outs.append(jnp.dot(diff.astype(BF16), w_ref[g].astype(BF16), preferred_element_type=F32))
    y = jnp.concatenate(outs, axis=-1) * scale_ref[...]
    o_ref[...] = y.astype(o_ref.dtype)


def _pool_prompt(p, w_pool, pool_scale, tt=256):
    B, T, _ = p.shape
    halo_blocks = tt // POOL_HALO
    return pl.pallas_call(
        functools.partial(_pool_prompt_kernel, tt=tt),
        grid=(B, T // tt),
        in_specs=[
            pl.BlockSpec((None, tt, D_POOL), lambda b, i: (b, i, 0)),
            pl.BlockSpec((None, POOL_HALO, D_POOL),
                         lambda b, i: (b, jnp.maximum(i * halo_blocks - 1, 0), 0)),
            pl.BlockSpec((len(POOL_WINDOWS), POOL_GROUP, POOL_GROUP), lambda b, i: (0, 0, 0)),
            pl.BlockSpec((1, D_POOL), lambda b, i: (0, 0)),
        ],
        out_specs=pl.BlockSpec((None, tt, D_POOL), lambda b, i: (b, i, 0)),
        out_shape=jax.ShapeDtypeStruct((B, T, D_POOL), BF16),
        scratch_shapes=[pltpu.VMEM((POOL_HALO + tt, D_POOL), F32)],
        compiler_params=_params("parallel", "arbitrary"),
        name="pool_prompt",
    )(p, p, w_pool, pool_scale.reshape(1, D_POOL))


def _stacked(layer, prev, shape):
    extra_in = [] if prev is None else [prev]
    extra_specs = [] if prev is None else [pl.BlockSpec(memory_space=pl.ANY)]
    return extra_in, extra_specs, jax.ShapeDtypeStruct(shape, F32)


def _chunk_prompt_kernel(u_ref, gv_ref, g_ref, b_ref, ws_ref, bs_ref, *rest):
    cv_ref, o_ref = rest[-2:]
    row = lax.broadcasted_iota(jnp.int32, (CHUNK, CHUNK), 0)
    col = lax.broadcasted_iota(jnp.int32, (CHUNK, CHUNK), 1)
    tril = col <= row
    for h in range(N_HEADS_CHUNK):
        cols = pl.ds(h * HEAD_DIM, HEAD_DIM)
        gvn = _layer_norm_rows(_gelu_tanh(gv_ref[:, cols]), g_ref[:, cols], b_ref[:, cols])
        cv_ref[:, h, :] = gvn
        w = jnp.where(tril, ws_ref[h], 0.0).astype(BF16)
        mixed = jnp.dot(w, gvn.astype(BF16), preferred_element_type=F32) + bs_ref[:, h:h + 1]
        o_ref[:, cols] = (_gelu_tanh(u_ref[:, cols]) * mixed).astype(o_ref.dtype)


def _chunk_prompt(p, ln_v_g, ln_v_b, w_s, b_s, layer, cv_prev):
    B, T, _ = p.shape
    vec = pl.BlockSpec((1, D_CHUNK), lambda b, i: (0, 0))
    extra_in, extra_specs, cv_shape = _stacked(
        layer, cv_prev, (DEPTH, B, T, N_HEADS_CHUNK, HEAD_DIM))
    n_in = 6
    return pl.pallas_call(
        _chunk_prompt_kernel,
        grid=(B, T // CHUNK),
        in_specs=[
            pl.BlockSpec((None, CHUNK, D_CHUNK), lambda b, i: (b, i, U_COL1024)),
            pl.BlockSpec((None, CHUNK, D_CHUNK), lambda b, i: (b, i, GV_COL1024)),
            vec, vec,
            pl.BlockSpec((N_HEADS_CHUNK, CHUNK, CHUNK), lambda b, i: (0, 0, 0)),
            pl.BlockSpec((CHUNK, N_HEADS_CHUNK), lambda b, i: (0, 0)),
        ] + extra_specs,
        out_specs=[
            pl.BlockSpec((None, None, CHUNK, N_HEADS_CHUNK, HEAD_DIM),
                         lambda b, i: (layer, b, i, 0, 0)),
            pl.BlockSpec((None, CHUNK, D_CHUNK), lambda b, i: (b, i, 0)),
        ],
        out_shape=[cv_shape, jax.ShapeDtypeStruct((B, T, D_CHUNK), BF16)],
        input_output_aliases={} if cv_prev is None else {n_in: 0},
        compiler_params=_params("parallel", "parallel"),
        name="chunk_prompt",
    )(p, p, ln_v_g.reshape(1, D_CHUNK), ln_v_b.reshape(1, D_CHUNK), w_s, b_s.T, *extra_in)


def _emit_kv_kernel(k_ref, v_ref, *rest):
    ko_ref, vo_ref = rest[-2:]
    half = N_HEADS_ATT // 2
    for h in range(half):
        cols = pl.ds(h * HEAD_DIM, HEAD_DIM)
        ko_ref[:, h, :] = k_ref[:, cols]
        vo_ref[:, h, :] = v_ref[:, cols]


def _emit_kv(p, layer, k_prev, v_prev, tt=256):
    B, T, _ = p.shape
    half = N_HEADS_ATT // 2
    shape = (DEPTH, B, T, N_HEADS_ATT, HEAD_DIM)
    extra_in = [] if k_prev is None else [k_prev, v_prev]
    any_spec = pl.BlockSpec(memory_space=pl.ANY)
    out_spec = pl.BlockSpec((None, None, tt, half, HEAD_DIM), lambda b, i, s: (layer, b, i, s, 0))
    return pl.pallas_call(
        _emit_kv_kernel,
        grid=(B, T // tt, 2),
        in_specs=[
            pl.BlockSpec((None, tt, half * HEAD_DIM), lambda b, i, s: (b, i, K_COL1024 + s)),
            pl.BlockSpec((None, tt, half * HEAD_DIM), lambda b, i, s: (b, i, V_COL1024 + s)),
        ] + [any_spec] * len(extra_in),
        out_specs=[out_spec, out_spec],
        out_shape=[jax.ShapeDtypeStruct(shape, F32), jax.ShapeDtypeStruct(shape, F32)],
        input_output_aliases={} if k_prev is None else {2: 0, 3: 1},
        compiler_params=_params("parallel", "parallel", "parallel"),
        name="emit_kv",
    )(p, p, *extra_in)


MASKED_LOGIT = -1e30


def _sb_scores(q, kb, bias, mask):
    z = lax.dot_general(q, kb, (((1,), (1,)), ((), ())), preferred_element_type=F32) + bias
    sp = _softplus(z)
    ls = z - sp
    if mask is not None:
        sp = jnp.where(mask, sp, 0.0)
        ls = jnp.where(mask, ls, MASKED_LOGIT)
    return ls, sp.astype(BF16)


def _sb_weights(ls, spb, vb, upper, ones, carry):
    x = ls - jnp.dot(spb, upper, preferred_element_type=F32)
    x = x - jnp.concatenate([carry] * (x.shape[1] // carry.shape[1]), axis=1)
    pv = jnp.dot(jnp.exp(x).astype(BF16), vb, preferred_element_type=F32)
    tot = jnp.dot(spb, ones, preferred_element_type=F32)
    return pv, tot


def _sb_prompt_kernel(bias_ref, q_ref, k_ref, v_ref, o_ref, kb_ref, vb_ref, acc_ref, carry_ref,
                      ls_ref, sp_ref, *, tq, hp):
    g = pl.program_id(1)
    i = pl.program_id(2)

    @pl.when(i == 0)
    def _():
        kb_ref[...] = k_ref[...].astype(BF16)
        vb_ref[...] = v_ref[...].astype(BF16)

    row = lax.broadcasted_iota(jnp.int32, (tq, tq), 0)
    col = lax.broadcasted_iota(jnp.int32, (tq, tq), 1)
    upper = (row > col).astype(BF16)
    ones = jnp.ones((tq, HEAD_DIM), BF16)
    qs = [(q_ref[:, h * HEAD_DIM:(h + 1) * HEAD_DIM] * ATT_SCALE).astype(BF16) for h in range(hp)]
    biases = [bias_ref[g * hp + h] for h in range(hp)]

    def key_rows(m):
        return pl.ds(pl.multiple_of((i - m) * tq, tq), tq)

    def stage1(m, mask):
        rows, slot = key_rows(m), m % 2
        for h in range(hp):
            ls, spb = _sb_scores(qs[h], kb_ref[rows, pl.ds(h * HEAD_DIM, HEAD_DIM)], biases[h], mask)
            ls_ref[slot, h] = ls
            sp_ref[slot, h] = spb

    def stage2(m):
        rows, slot = key_rows(m), m % 2
        for h in range(hp):
            pv, tot = _sb_weights(ls_ref[slot, h], sp_ref[slot, h],
                                  vb_ref[rows, pl.ds(h * HEAD_DIM, HEAD_DIM)], upper, ones, carry_ref[h])
            acc_ref[h] += pv
            carry_ref[h] += tot

    acc_ref[...] = jnp.zeros_like(acc_ref)
    carry_ref[...] = jnp.zeros_like(carry_ref)
    stage1(0, col < row)

    def body(m, c):
        stage2(m - 1)
        stage1(m, None)
        return c

    lax.fori_loop(1, i + 1, body, 0)
    stage2(i)
    for h in range(hp):
        o_ref[:, h * HEAD_DIM:(h + 1) * HEAD_DIM] = acc_ref[h].astype(o_ref.dtype)


def _sb_prompt(p, sb_bias, tq=256, hp=4):
    B, T, _ = p.shape
    w = hp * HEAD_DIM
    return pl.pallas_call(
        functools.partial(_sb_prompt_kernel, tq=tq, hp=hp),
        grid=(B, N_HEADS_ATT // hp, T // tq),
        in_specs=[
            pl.BlockSpec(memory_space=pltpu.SMEM),
            pl.BlockSpec((None, tq, w), lambda b, g, i: (b, i, Q_COL128 // hp + g)),
            pl.BlockSpec((None, T, w), lambda b, g, i: (b, 0, K_COL128 // hp + g)),
            pl.BlockSpec((None, T, w), lambda b, g, i: (b, 0, V_COL128 // hp + g)),
        ],
        out_specs=pl.BlockSpec((None, tq, w), lambda b, g, i: (b, i, g)),
        out_shape=jax.ShapeDtypeStruct((B, T, D_ATT), BF16),
        scratch_shapes=[
            pltpu.VMEM((T, w), BF16), pltpu.VMEM((T, w), BF16),
            pltpu.VMEM((hp, tq, HEAD_DIM), F32), pltpu.VMEM((hp, tq, HEAD_DIM), F32),
            pltpu.VMEM((2, hp, tq, tq), F32), pltpu.VMEM((2, hp, tq, tq), BF16),
        ],
        compiler_params=_params("parallel", "parallel", "arbitrary"),
        name="sb_prompt",
    )(sb_bias, p, p, p)


def _sb_sample_kernel(pt_ref, bias_ref, q_ref, *refs, n_steps, pps, page):
    del pt_ref
    k_refs, v_refs = refs[:pps], refs[pps:2 * pps]
    o_ref, carry_ref, acc_ref = refs[2 * pps:]
    j = pl.program_id(1)
    H = N_HEADS_ATT
    n = page * H
    lane = lax.broadcasted_iota(jnp.int32, (1, n), 1)
    own = (lax.broadcasted_iota(jnp.int32, (H, n), 1) % H) == lax.broadcasted_iota(jnp.int32, (H, n), 0)

    @pl.when(j == 0)
    def _():
        carry_ref[...] = jnp.zeros_like(carry_ref)
        acc_ref[...] = jnp.zeros_like(acc_ref)

    q = (q_ref[...] * ATT_SCALE).astype(BF16)
    carry = carry_ref[...]
    acc = acc_ref[...]
    for pg in reversed(range(pps)):
        kb = k_refs[pg][...].reshape(n, HEAD_DIM).astype(BF16)
        vb = v_refs[pg][...].reshape(n, HEAD_DIM).astype(BF16)
        zt = lax.dot_general(q, kb, (((1,), (1,)), ((), ())), preferred_element_type=F32)
        z = jnp.sum(jnp.where(own, zt, 0.0), axis=0, keepdims=True) + bias_ref[...]
        sp = _softplus(z)
        later = jnp.where(lane < n - H, pltpu.roll(sp, n - H, 1), 0.0)
        total = sp
        step = H
        while step < n:
            later = later + jnp.where(lane < n - step, pltpu.roll(later, n - step, 1), 0.0)
            total = total + pltpu.roll(total, step, 1)
            step *= 2
        w = jnp.exp(z - sp - later - carry)
        wm = jnp.where(own, jnp.broadcast_to(w, (H, n)), 0.0).astype(BF16)
        acc = acc + jnp.dot(wm, vb, preferred_element_type=F32)
        carry = carry + total
    carry_ref[...] = carry
    acc_ref[...] = acc

    @pl.when(j == n_steps - 1)
    def _():
        o_ref[...] = acc


def _sb_sample(q, cache_k, cache_v, page_table, sb_bias, layer, pps=4):
    n_seq, n_pages = page_table.shape
    page = cache_k.shape[2]
    n_steps = n_pages // pps
    H = N_HEADS_ATT

    def cache_spec(pg):
        return pl.BlockSpec(
            (None, None, page, H, HEAD_DIM),
            lambda b, j, pt: (layer, pt[b, n_pages - (j + 1) * pps + pg], 0, 0, 0))

    grid_spec = pltpu.PrefetchScalarGridSpec(
        num_scalar_prefetch=1,
        grid=(n_seq, n_steps),
        in_specs=[
            pl.BlockSpec((1, page * H), lambda b, j, pt: (0, 0)),
            pl.BlockSpec((None, H, HEAD_DIM), lambda b, j, pt: (b, 0, 0)),
        ] + [cache_spec(pg) for pg in range(pps)] * 2,
        out_specs=pl.BlockSpec((None, H, HEAD_DIM), lambda b, j, pt: (b, 0, 0)),
        scratch_shapes=[pltpu.VMEM((1, page * H), F32), pltpu.VMEM((H, HEAD_DIM), F32)],
    )
    bias_row = jnp.tile(sb_bias, page).reshape(1, page * H)
    return pl.pallas_call(
        functools.partial(_sb_sample_kernel, n_steps=n_steps, pps=pps, page=page),
        grid_spec=grid_spec,
        out_shape=jax.ShapeDtypeStruct((n_seq, H, HEAD_DIM), F32),
        compiler_params=_params("parallel", "arbitrary"),
        name="sb_sample",
    )(page_table, bias_row, q, *([cache_k] * pps), *([cache_v] * pps))


def _mix_sample_kernel(p_ref, att_ref, state_ref, wp_ref, scale_ref, g_ref, b_ref, w0_ref, b0_ref,
                       cv_ref, o_ref):
    xa = p_ref[:, 0:D_POOL]
    outs = []
    for g, win in enumerate(POOL_WINDOWS):
        lo, hi = g * POOL_GROUP, (g + 1) * POOL_GROUP
        acc = xa[:, lo:hi]
        for back in range(1, win):
            acc = acc + state_ref[POOL_BUF - back, :, lo:hi]
        diff = acc / float(win) - xa[:, lo:hi]
        outs.append(jnp.dot(diff.astype(BF16), wp_ref[g].astype(BF16), preferred_element_type=F32))
    o_ref[:, 0:D_POOL] = (jnp.concatenate(outs, axis=-1) * scale_ref[...]).astype(o_ref.dtype)
    o_ref[:, D_POOL:D_POOL + D_ATT] = att_ref[...].astype(o_ref.dtype)
    u0 = D_POOL + 3 * D_ATT
    for h in range(N_HEADS_CHUNK):
        lo, hi = h * HEAD_DIM, (h + 1) * HEAD_DIM
        gvn = _layer_norm_rows(_gelu_tanh(p_ref[:, u0 + D_CHUNK + lo:u0 + D_CHUNK + hi]),
                               g_ref[:, lo:hi], b_ref[:, lo:hi])
        cv_ref[:, lo:hi] = gvn
        mixed = w0_ref[:, lo:hi] * gvn + b0_ref[:, lo:hi]
        o_ref[:, D_POOL + D_ATT + lo:D_POOL + D_ATT + hi] = (
            _gelu_tanh(p_ref[:, u0 + lo:u0 + hi]) * mixed).astype(o_ref.dtype)


def _mix_sample(p, att, state, w_pool, pool_scale, ln_v_g, ln_v_b, w_s, b_s):
    S = p.shape[0]
    w0 = jnp.repeat(w_s[:, 0, 0], HEAD_DIM).reshape(1, D_CHUNK)
    b0 = jnp.repeat(b_s[:, 0], HEAD_DIM).reshape(1, D_CHUNK)
    return pl.pallas_call(
        _mix_sample_kernel,
        out_shape=[
            jax.ShapeDtypeStruct((S, D_CHUNK), F32),
            jax.ShapeDtypeStruct((S, D_MODEL), F32),
        ],
        compiler_params=pltpu.CompilerParams(vmem_limit_bytes=VMEM_LIMIT_BYTES),
        name="mix_sample",
    )(p, att, jnp.swapaxes(state, 0, 1), w_pool, pool_scale.reshape(1, D_POOL), ln_v_g.reshape(1, D_CHUNK),
      ln_v_b.reshape(1, D_CHUNK), w0, b0)


def kernel(x_prompt, x_sample, cache_k, cache_v, state_pool, page_table, c_prompt, c_sample,
           ln0_g, ln0_b, w_ada, b_ada, w_in, sb_bias, w_pool, pool_scale, ln_v_g, ln_v_b, w_s, b_s,
           w_out, ln1_g, ln1_b, w_ff1, w_ff2, ln2_g, ln2_b):
    B, T, D = x_prompt.shape
    S = x_sample.shape[0]
    H = N_HEADS_ATT
    w_ff2_b = w_ff2.astype(BF16)

    pad = (-(B + S)) % 8
    cs = jnp.concatenate([c_prompt, c_sample, jnp.zeros((pad, D), F32)], axis=0)
    mod = _ada(cs, w_ada, b_ada)
    mod_p = [mod[l, :B].reshape(B, 1, 6 * D) for l in range(DEPTH)]
    mod_s = [mod[l, B:B + S].reshape(1, S, 6 * D) for l in range(DEPTH)]
    xs = x_sample.reshape(1, S, D)

    xp, hp = _ln_mod(x_prompt, ln0_g, ln0_b, mod=mod_p[0], sc_col=1)
    xs, hs = _ln_mod(xs, ln0_g, ln0_b, mod=mod_s[0], sc_col=1, h_dtype=F32)

    k_st = v_st = cv_st = None
    outs = {k: [] for k in ("pp", "ks", "vs", "ps", "cs")}
    for l in range(DEPTH):
        p, ps = _dense(hp.reshape(B * T, D), hs.reshape(S, D), w_in[l])
        p = p.reshape(B, T, D_IN)

        a_out = _pool_prompt(p, w_pool[l], pool_scale[l])
        b_out = _sb_prompt(p, sb_bias[l])
        cv_st, c_out = _chunk_prompt(p, ln_v_g[l], ln_v_b[l], w_s[l], b_s[l], l, cv_st)
        k_st, v_st = _emit_kv(p, l, k_st, v_st)
        outs["pp"].append(p[:, T - POOL_BUF:, :D_POOL])
        mix = jnp.concatenate([a_out, b_out, c_out], axis=-1).reshape(B * T, D)

        q_s = ps[:, D_POOL:D_POOL + D_ATT].reshape(S, H, HEAD_DIM)
        att = _sb_sample(q_s, cache_k, cache_v, page_table, sb_bias[l], l).reshape(S, D_ATT)
        cv_s, mix_s = _mix_sample(ps, att, state_pool[l], w_pool[l], pool_scale[l], ln_v_g[l],
                                  ln_v_b[l], w_s[l], b_s[l])
        outs["ks"].append(ps[:, D_POOL + D_ATT:D_POOL + 2 * D_ATT].reshape(S, 1, H, HEAD_DIM))
        outs["vs"].append(ps[:, D_POOL + 2 * D_ATT:D_POOL + 3 * D_ATT].reshape(S, 1, H, HEAD_DIM))
        outs["ps"].append(jnp.concatenate([state_pool[l][:, 1:], ps[:, None, :D_POOL]], axis=1))
        outs["cs"].append(cv_s.reshape(S, 1, N_HEADS_CHUNK, HEAD_DIM))

        mo, mo_s = _dense(mix, mix_s, w_out[l])
        xp, h2p = _ln_mod(xp, ln1_g[l], ln1_b[l], y=mo.reshape(B, T, D), mod_res=mod_p[l],
                          gate_col=2, mod=mod_p[l], sc_col=4)
        xs, h2s = _ln_mod(xs, ln1_g[l], ln1_b[l], y=mo_s.reshape(1, S, D), mod_res=mod_s[l],
                          gate_col=2, mod=mod_s[l], sc_col=4, h_dtype=F32)

        nxt = l + 1 < DEPTH
        a, a_s = _dense(h2p.reshape(B * T, D), h2s.reshape(S, D), w_ff1[l], out_dtype=BF16, relu2=True)
        f = _matmul(a, w_ff2_b[l], tm=1024, tn=1024, tk=2048).reshape(B, T, D)
        f_s = _matmul(a_s, w_ff2_b[l], tm=S, tn=1024, tk=D).reshape(1, S, D)
        xp, hp = _ln_mod(xp, ln2_g[l], ln2_b[l], y=f, mod_res=mod_p[l], gate_col=5,
                         mod=mod_p[l + 1] if nxt else None, sc_col=1)
        xs, hs = _ln_mod(xs, ln2_g[l], ln2_b[l], y=f_s, mod_res=mod_s[l], gate_col=5,
                         mod=mod_s[l + 1] if nxt else None, sc_col=1, h_dtype=F32)

    st = {k: jnp.stack(v) for k, v in outs.items()}
    return (xp, xs.reshape(S, 1, D), k_st, v_st, st["pp"], cv_st,
            st["ks"], st["vs"], st["ps"], st["cs"])
```

```python
import functools
import math

import jax
import jax.numpy as jnp
from jax import lax
from jax.experimental import pallas as pl
from jax.experimental.pallas import tpu as pltpu

F32 = jnp.float32
BF16 = jnp.bfloat16

D_MODEL = 4096
HEAD_DIM = 128
D_ATT = D_MODEL // 2
D_POOL = D_MODEL // 4
D_CHUNK = D_MODEL // 4
N_HEADS_ATT = D_ATT // HEAD_DIM
N_HEADS_CHUNK = D_CHUNK // HEAD_DIM
POOL_WINDOWS = (2, 4, 8, 16)
POOL_GROUP = D_POOL // len(POOL_WINDOWS)
POOL_BUF = max(POOL_WINDOWS) - 1
POOL_HALO = 16
CHUNK = 128
D_FF = 4 * D_MODEL
D_IN = D_POOL + 3 * D_ATT + 2 * D_CHUNK
DEPTH = 2
ALPHA = (2.0 * DEPTH) ** 0.25
LN_EPS = 1e-5
ATT_SCALE = 1.0 / math.sqrt(HEAD_DIM)

Q_COL128 = D_POOL // HEAD_DIM
K_COL128 = (D_POOL + D_ATT) // HEAD_DIM
V_COL128 = (D_POOL + 2 * D_ATT) // HEAD_DIM
K_COL1024 = (D_POOL + D_ATT) // 1024
V_COL1024 = (D_POOL + 2 * D_ATT) // 1024
U_COL1024 = (D_POOL + 3 * D_ATT) // D_CHUNK
GV_COL1024 = U_COL1024 + 1

VMEM_LIMIT_BYTES = 56 * 1024 * 1024


def _params(*sem):
    return pltpu.CompilerParams(dimension_semantics=sem, vmem_limit_bytes=VMEM_LIMIT_BYTES)


def _gelu_tanh(x):
    c = math.sqrt(2.0 / math.pi)
    return 0.5 * x * (1.0 + jnp.tanh(c * (x + 0.044715 * (x * x * x))))


def _softplus(z):
    return jnp.maximum(z, 0.0) + jnp.log(1.0 + jnp.exp(-jnp.abs(z)))


def _layer_norm_rows(x, g, b):
    mu = jnp.mean(x, axis=-1, keepdims=True)
    xc = x - mu
    var = jnp.mean(xc * xc, axis=-1, keepdims=True)
    return xc * lax.rsqrt(var + LN_EPS) * g + b


def _ada_kernel(c_ref, w_ref, b_ref, o_ref):
    c = c_ref[...]
    s = (c * jax.nn.sigmoid(c)).astype(BF16)
    o_ref[...] = jnp.dot(s, w_ref[...].astype(BF16), preferred_element_type=F32) + b_ref[...]


def _ada(cs, w_ada, b_ada, tn=512):
    rows = cs.shape[0]
    n = w_ada.shape[-1]
    return pl.pallas_call(
        _ada_kernel,
        grid=(DEPTH, n // tn),
        in_specs=[
            pl.BlockSpec((rows, D_MODEL), lambda l, j: (0, 0)),
            pl.BlockSpec((None, D_MODEL, tn), lambda l, j: (l, 0, j)),
            pl.BlockSpec((None, 1, tn), lambda l, j: (l, 0, j)),
        ],
        out_specs=pl.BlockSpec((None, rows, tn), lambda l, j: (l, 0, j)),
        out_shape=jax.ShapeDtypeStruct((DEPTH, rows, n), F32),
        compiler_params=_params("arbitrary", "arbitrary"),
        name="ada_mod",
    )(cs, w_ada, b_ada.reshape(DEPTH, 1, n))


def _ln_mod_kernel(*refs, has_res, has_mod):
    it = iter(refs)
    x_ref = next(it)
    if has_res:
        y_ref, gate_ref = next(it), next(it)
    g_ref, b_ref = next(it), next(it)
    if has_mod:
        sc_ref, sh_ref = next(it), next(it)
    xo_ref = next(it)
    x = x_ref[...]
    if has_res:
        x = ALPHA * x + gate_ref[...] * y_ref[...].astype(F32)
    xn = _layer_norm_rows(x, g_ref[...], b_ref[...])
    xo_ref[...] = xn
    if has_mod:
        ho_ref = next(it)
        ho_ref[...] = (xn * (1.0 + sc_ref[...]) + sh_ref[...]).astype(ho_ref.dtype)


def _ln_mod(x, ln_g, ln_b, *, y=None, mod=None, gate_col=None, mod_res=None, sc_col=None,
            h_dtype=BF16, tt=256):
    B, T, D = x.shape
    tt = min(tt, T)
    has_res = y is not None
    has_mod = mod is not None
    row = pl.BlockSpec((None, tt, D), lambda b, i: (b, i, 0))
    vec = pl.BlockSpec((1, D), lambda b, i: (0, 0))

    def modspec(m, col):
        tm = m.shape[1]
        return pl.BlockSpec((None, tm, D), lambda b, i: (b, 0, col))

    args, specs = [x], [row]
    if has_res:
        args += [y, mod_res]
        specs += [row, modspec(mod_res, gate_col)]
    args += [ln_g.reshape(1, D), ln_b.reshape(1, D)]
    specs += [vec, vec]
    if has_mod:
        args += [mod, mod]
        specs += [modspec(mod, sc_col), modspec(mod, sc_col - 1)]
    out_shape = [jax.ShapeDtypeStruct((B, T, D), F32)]
    out_specs = [row]
    if has_mod:
        out_shape.append(jax.ShapeDtypeStruct((B, T, D), h_dtype))
        out_specs.append(row)
    res = pl.pallas_call(
        functools.partial(_ln_mod_kernel, has_res=has_res, has_mod=has_mod),
        grid=(B, T // tt),
        in_specs=specs,
        out_specs=out_specs,
        out_shape=out_shape,
        compiler_params=_params("parallel", "parallel"),
        name="ln_mod",
    )(*args)
    return (res[0], res[1]) if has_mod else (res[0], None)


def _relu2(acc):
    r = jnp.maximum(acc, 0.0)
    return r * r


def _cast_jobs(jobs, step_of):
    in_specs, out_specs, out_shapes, counts = [], [], [], []
    for src, layer, rb, cb in jobs:
        _, rows, cols = src.shape
        ncb = cols // cb
        count = (rows // rb) * ncb

        def block(*g, count=count, ncb=ncb):
            s = jnp.minimum(step_of(*g), count - 1)
            return s // ncb, s % ncb

        in_specs.append(pl.BlockSpec((None, rb, cb),
                                     lambda *g, block=block, layer=layer: (layer, *block(*g))))
        out_specs.append(pl.BlockSpec((rb, cb), lambda *g, block=block: block(*g)))
        out_shapes.append(jax.ShapeDtypeStruct((rows, cols), BF16))
        counts.append(count)
    return in_specs, out_specs, out_shapes, counts


def _run_casts(step, srcs, dsts, counts):
    for src, dst, count in zip(srcs, dsts, counts):
        @pl.when(step < count)
        def _(src=src, dst=dst):
            dst[...] = src[...].astype(BF16)


def _dense_kernel(x_ref, xs_ref, w_ref, *rest, relu2, counts, ni):
    nc = len(counts)
    srcs, (o_ref, os_ref), dsts = rest[:nc], rest[nc:nc + 2], rest[nc + 2:]
    post = _relu2 if relu2 else (lambda a: a)
    j, i = pl.program_id(0), pl.program_id(1)

    @pl.when(i == 0)
    def _():
        os_ref[...] = post(jnp.dot(xs_ref[...].astype(BF16), w_ref[...],
                                   preferred_element_type=F32)).astype(os_ref.dtype)

    o_ref[...] = post(jnp.dot(x_ref[...], w_ref[...],
                              preferred_element_type=F32)).astype(o_ref.dtype)
    _run_casts(j * ni + i, srcs, dsts, counts)


def _dense(x, xs, w, *, out_dtype=F32, relu2=False, casts=(), tm=1024, tn=1024):
    M, K = x.shape
    S = xs.shape[0]
    N = w.shape[1]
    ni = M // tm
    c_in, c_out, c_shapes, counts = _cast_jobs(casts, lambda j, i: j * ni + i)
    return pl.pallas_call(
        functools.partial(_dense_kernel, relu2=relu2, counts=tuple(counts), ni=ni),
        grid=(N // tn, ni),
        in_specs=[
            pl.BlockSpec((tm, K), lambda j, i: (i, 0)),
            pl.BlockSpec((S, K), lambda j, i: (0, 0)),
            pl.BlockSpec((K, tn), lambda j, i: (0, j)),
        ] + c_in,
        out_specs=[
            pl.BlockSpec((tm, tn), lambda j, i: (i, j)),
            pl.BlockSpec((S, tn), lambda j, i: (0, j)),
        ] + c_out,
        out_shape=[jax.ShapeDtypeStruct((M, N), out_dtype), jax.ShapeDtypeStruct((S, N), F32)] + c_shapes,
        compiler_params=_params("arbitrary", "arbitrary"),
        name="dense",
    )(x, xs, w, *[job[0] for job in casts])


def _mm_kernel(x_ref, w_ref, *rest, nk, nj, relu2, counts):
    nc = len(counts)
    srcs, o_ref, dsts = rest[:nc], rest[nc], rest[nc + 1:2 * nc + 1]
    i, j, k = pl.program_id(0), pl.program_id(1), pl.program_id(2)

    def finish(acc):
        o_ref[...] = (_relu2(acc) if relu2 else acc).astype(o_ref.dtype)

    if nk == 1:
        finish(jnp.dot(x_ref[...].astype(BF16), w_ref[...], preferred_element_type=F32))
    else:
        acc_ref = o_ref if (o_ref.dtype == F32 and not relu2) else rest[-1]

        @pl.when(k == 0)
        def _():
            acc_ref[...] = jnp.zeros_like(acc_ref)

        acc_ref[...] += jnp.dot(x_ref[...].astype(BF16), w_ref[...], preferred_element_type=F32)

        if acc_ref is not o_ref:
            @pl.when(k == nk - 1)
            def _():
                finish(acc_ref[...])

    _run_casts((i * nj + j) * nk + k, srcs, dsts, counts)


def _matmul(x, w, *, tm, tn, tk, out_dtype=F32, relu2=False, casts=()):
    M, K = x.shape
    N = w.shape[1]
    tm, tn, tk = min(tm, M), min(tn, N), min(tk, K)
    nj, nk = N // tn, K // tk
    needs_acc = nk > 1 and (out_dtype != F32 or relu2)
    scratch = [pltpu.VMEM((tm, tn), F32)] if needs_acc else []
    c_in, c_out, c_shapes, counts = _cast_jobs(casts, lambda i, j, k: (i * nj + j) * nk + k)
    res = pl.pallas_call(
        functools.partial(_mm_kernel, nk=nk, nj=nj, relu2=relu2, counts=tuple(counts)),
        grid=(M // tm, nj, nk),
        in_specs=[
            pl.BlockSpec((tm, tk), lambda i, j, k: (i, k)),
            pl.BlockSpec((tk, tn), lambda i, j, k: (k, j)),
        ] + c_in,
        out_specs=[pl.BlockSpec((tm, tn), lambda i, j, k: (i, j))] + c_out,
        out_shape=[jax.ShapeDtypeStruct((M, N), out_dtype)] + c_shapes,
        scratch_shapes=scratch,
        compiler_params=_params("arbitrary", "arbitrary", "arbitrary"),
        name="dense_matmul",
    )(x, w, *[job[0] for job in casts])
    return res if casts else res[0]


def _pool_prompt_kernel(xa_ref, halo_ref, w_ref, scale_ref, o_ref, ext_ref, *, tt):
    i = pl.program_id(1)
    xa = xa_ref[...]
    ext_ref[pl.ds(POOL_HALO, tt), :] = xa

    @pl.when(i == 0)
    def _():
        ext_ref[pl.ds(0, POOL_HALO), :] = jnp.zeros((POOL_HALO, D_POOL), F32)

    @pl.when(i > 0)
    def _():
        ext_ref[pl.ds(0, POOL_HALO), :] = halo_ref[...]

    pos = i * tt + lax.broadcasted_iota(jnp.int32, (tt, 1), 0)
    outs = []
    for g, win in enumerate(POOL_WINDOWS):
        cols = pl.ds(g * POOL_GROUP, POOL_GROUP)
        acc = ext_ref[pl.ds(POOL_HALO, tt), cols]
        for back in range(1, win):
            acc = acc + ext_ref[pl.ds(POOL_HALO - back, tt), cols]
        cnt = jnp.minimum(win, pos + 1).astype(F32)
        diff = acc / cnt - xa[:, g * POOL_GROUP:(g + 1) * POOL_GROUP]
        outs.append(jnp.dot(diff.astype(BF16), w_ref[g].astype(BF16), preferred_element_type=F32))
    y = jnp.concatenate(outs, axis=-1) * scale_ref[...]
    o_ref[...] = y.astype(o_ref.dtype)


def _pool_prompt(p, w_pool, pool_scale, tt=256):
    B, T, _ = p.shape
    halo_blocks = tt // POOL_HALO
    return pl.pallas_call(
        functools.partial(_pool_prompt_kernel, tt=tt),
        grid=(B, T // tt),
        in_specs=[
            pl.BlockSpec((None, tt, D_POOL), lambda b, i: (b, i, 0)),
            pl.BlockSpec((None, POOL_HALO, D_POOL),
                         lambda b, i: (b, jnp.maximum(i * halo_blocks - 1, 0), 0)),
            pl.BlockSpec((len(POOL_WINDOWS), POOL_GROUP, POOL_GROUP), lambda b, i: (0, 0, 0)),
            pl.BlockSpec((1, D_POOL), lambda b, i: (0, 0)),
        ],
        out_specs=pl.BlockSpec((None, tt, D_POOL), lambda b, i: (b, i, 0)),
        out_shape=jax.ShapeDtypeStruct((B, T, D_POOL), BF16),
        scratch_shapes=[pltpu.VMEM((POOL_HALO + tt, D_POOL), F32)],
        compiler_params=_params("parallel", "arbitrary"),
        name="pool_prompt",
    )(p, p, w_pool, pool_scale.reshape(1, D_POOL))


def _stacked(layer, prev, shape):
    extra_in = [] if prev is None else [prev]
    extra_specs = [] if prev is None else [pl.BlockSpec(memory_space=pl.ANY)]
    return extra_in, extra_specs, jax.ShapeDtypeStruct(shape, F32)


def _chunk_prompt_kernel(u_ref, gv_ref, g_ref, b_ref, ws_ref, bs_ref, *rest):
    cv_ref, o_ref = rest[-2:]
    row = lax.broadcasted_iota(jnp.int32, (CHUNK, CHUNK), 0)
    col = lax.broadcasted_iota(jnp.int32, (CHUNK, CHUNK), 1)
    tril = col <= row
    for h in range(N_HEADS_CHUNK):
        cols = pl.ds(h * HEAD_DIM, HEAD_DIM)
        gvn = _layer_norm_rows(_gelu_tanh(gv_ref[:, cols]), g_ref[:, cols], b_ref[:, cols])
        cv_ref[:, h, :] = gvn
        w = jnp.where(tril, ws_ref[h], 0.0).astype(BF16)
        mixed = jnp.dot(w, gvn.astype(BF16), preferred_element_type=F32) + bs_ref[:, h:h + 1]
        o_ref[:, cols] = (_gelu_tanh(u_ref[:, cols]) * mixed).astype(o_ref.dtype)


def _chunk_prompt(p, ln_v_g, ln_v_b, w_s, b_s, layer, cv_prev):
    B, T, _ = p.shape
    vec = pl.BlockSpec((1, D_CHUNK), lambda b, i: (0, 0))
    extra_in, extra_specs, cv_shape = _stacked(
        layer, cv_prev, (DEPTH, B, T, N_HEADS_CHUNK, HEAD_DIM))
    n_in = 6
    return pl.pallas_call(
        _chunk_prompt_kernel,
        grid=(B, T // CHUNK),
        in_specs=[
            pl.BlockSpec((None, CHUNK, D_CHUNK), lambda b, i: (b, i, U_COL1024)),
            pl.BlockSpec((None, CHUNK, D_CHUNK), lambda b, i: (b, i, GV_COL1024)),
            vec, vec,
            pl.BlockSpec((N_HEADS_CHUNK, CHUNK, CHUNK), lambda b, i: (0, 0, 0)),
            pl.BlockSpec((CHUNK, N_HEADS_CHUNK), lambda b, i: (0, 0)),
        ] + extra_specs,
        out_specs=[
            pl.BlockSpec((None, None, CHUNK, N_HEADS_CHUNK, HEAD_DIM),
                         lambda b, i: (layer, b, i, 0, 0)),
            pl.BlockSpec((None, CHUNK, D_CHUNK), lambda b, i: (b, i, 0)),
        ],
        out_shape=[cv_shape, jax.ShapeDtypeStruct((B, T, D_CHUNK), BF16)],
        input_output_aliases={} if cv_prev is None else {n_in: 0},
        compiler_params=_params("parallel", "parallel"),
        name="chunk_prompt",
    )(p, p, ln_v_g.reshape(1, D_CHUNK), ln_v_b.reshape(1, D_CHUNK), w_s, b_s.T, *extra_in)


def _emit_kv_kernel(k_ref, v_ref, *rest):
    ko_ref, vo_ref = rest[-2:]
    half = N_HEADS_ATT // 2
    for h in range(half):
        cols = pl.ds(h * HEAD_DIM, HEAD_DIM)
        ko_ref[:, h, :] = k_ref[:, cols]
        vo_ref[:, h, :] = v_ref[:, cols]


def _emit_kv(p, layer, k_prev, v_prev, tt=256):
    B, T, _ = p.shape
    half = N_HEADS_ATT // 2
    shape = (DEPTH, B, T, N_HEADS_ATT, HEAD_DIM)
    extra_in = [] if k_prev is None else [k_prev, v_prev]
    any_spec = pl.BlockSpec(memory_space=pl.ANY)
    out_spec = pl.BlockSpec((None, None, tt, half, HEAD_DIM), lambda b, i, s: (layer, b, i, s, 0))
    return pl.pallas_call(
        _emit_kv_kernel,
        grid=(B, T // tt, 2),
        in_specs=[
            pl.BlockSpec((None, tt, half * HEAD_DIM), lambda b, i, s: (b, i, K_COL1024 + s)),
            pl.BlockSpec((None, tt, half * HEAD_DIM), lambda b, i, s: (b, i, V_COL1024 + s)),
        ] + [any_spec] * len(extra_in),
        out_specs=[out_spec, out_spec],
        out_shape=[jax.ShapeDtypeStruct(shape, F32), jax.ShapeDtypeStruct(shape, F32)],
        input_output_aliases={} if k_prev is None else {2: 0, 3: 1},
        compiler_params=_params("parallel", "parallel", "parallel"),
        name="emit_kv",
    )(p, p, *extra_in)


MASKED_LOGIT = -1e30


def _sb_scores(q, kb, bias, mask):
    z = lax.dot_general(q, kb, (((1,), (1,)), ((), ())), preferred_element_type=F32) + bias
    sp = _softplus(z)
    ls = z - sp
    if mask is not None:
        sp = jnp.where(mask, sp, 0.0)
        ls = jnp.where(mask, ls, MASKED_LOGIT)
    return ls, sp.astype(BF16)


def _sb_weights(ls, spb, vb, upper, ones, carry):
    later = jnp.dot(spb, upper, preferred_element_type=F32)
    x = ls - later - jnp.concatenate([carry] * (ls.shape[1] // carry.shape[1]), axis=1)
    pv = jnp.dot(jnp.exp(x).astype(BF16), vb, preferred_element_type=F32)
    del ones
    tot = jnp.broadcast_to(later[:, :1] + spb[:, :1].astype(F32), carry.shape)
    return pv, tot


def _sb_prompt_kernel(bias_ref, q_ref, k_ref, v_ref, o_ref, kb_ref, vb_ref, acc_ref, carry_ref,
                      ls_ref, sp_ref, *, tq, hp):
    g = pl.program_id(1)
    i = pl.program_id(2)

    @pl.when(i == 0)
    def _():
        kb_ref[...] = k_ref[...].astype(BF16)
        vb_ref[...] = v_ref[...].astype(BF16)

    row = lax.broadcasted_iota(jnp.int32, (tq, tq), 0)
    col = lax.broadcasted_iota(jnp.int32, (tq, tq), 1)
    upper = (row > col).astype(BF16)
    ones = jnp.ones((tq, HEAD_DIM), BF16)
    qs = [(q_ref[:, h * HEAD_DIM:(h + 1) * HEAD_DIM] * ATT_SCALE).astype(BF16) for h in range(hp)]
    biases = [bias_ref[g * hp + h] for h in range(hp)]

    def key_rows(m):
        return pl.ds(pl.multiple_of((i - m) * tq, tq), tq)

    def stage1(m, mask):
        rows, slot = key_rows(m), m % 2
        for h in range(hp):
            ls, spb = _sb_scores(qs[h], kb_ref[rows, pl.ds(h * HEAD_DIM, HEAD_DIM)], biases[h], mask)
            ls_ref[slot, h] = ls
            sp_ref[slot, h] = spb

    def stage2(m):
        rows, slot = key_rows(m), m % 2
        for h in range(hp):
            pv, tot = _sb_weights(ls_ref[slot, h], sp_ref[slot, h],
                                  vb_ref[rows, pl.ds(h * HEAD_DIM, HEAD_DIM)], upper, ones, carry_ref[h])
            acc_ref[h] += pv
            carry_ref[h] += tot

    acc_ref[...] = jnp.zeros_like(acc_ref)
    carry_ref[...] = jnp.zeros_like(carry_ref)
    stage1(0, col < row)

    def body(m, c):
        stage2(m - 1)
        stage1(m, None)
        return c

    lax.fori_loop(1, i + 1, body, 0)
    stage2(i)
    for h in range(hp):
        o_ref[:, h * HEAD_DIM:(h + 1) * HEAD_DIM] = acc_ref[h].astype(o_ref.dtype)


def _sb_prompt(p, sb_bias, tq=256, hp=4):
    B, T, _ = p.shape
    w = hp * HEAD_DIM
    return pl.pallas_call(
        functools.partial(_sb_prompt_kernel, tq=tq, hp=hp),
        grid=(B, N_HEADS_ATT // hp, T // tq),
        in_specs=[
            pl.BlockSpec(memory_space=pltpu.SMEM),
            pl.BlockSpec((None, tq, w), lambda b, g, i: (b, i, Q_COL128 // hp + g)),
            pl.BlockSpec((None, T, w), lambda b, g, i: (b, 0, K_COL128 // hp + g)),
            pl.BlockSpec((None, T, w), lambda b, g, i: (b, 0, V_COL128 // hp + g)),
        ],
        out_specs=pl.BlockSpec((None, tq, w), lambda b, g, i: (b, i, g)),
        out_shape=jax.ShapeDtypeStruct((B, T, D_ATT), BF16),
        scratch_shapes=[
            pltpu.VMEM((T, w), BF16), pltpu.VMEM((T, w), BF16),
            pltpu.VMEM((hp, tq, HEAD_DIM), F32), pltpu.VMEM((hp, tq, HEAD_DIM), F32),
            pltpu.VMEM((2, hp, tq, tq), F32), pltpu.VMEM((2, hp, tq, tq), BF16),
        ],
        compiler_params=_params("parallel", "parallel", "arbitrary"),
        name="sb_prompt",
    )(sb_bias, p, p, p)


def _sb_sample_kernel(pt_ref, bias_ref, q_ref, *refs, n_steps, pps, page):
    del pt_ref
    k_refs, v_refs = refs[:pps], refs[pps:2 * pps]
    o_ref, carry_ref, acc_ref = refs[2 * pps:]
    j = pl.program_id(1)
    H = N_HEADS_ATT
    n = page * H
    lane = lax.broadcasted_iota(jnp.int32, (pps, n), 1)
    own = (lax.broadcasted_iota(jnp.int32, (H, n), 1) % H) == lax.broadcasted_iota(jnp.int32, (H, n), 0)

    @pl.when(j == 0)
    def _():
        carry_ref[...] = jnp.zeros_like(carry_ref)
        acc_ref[...] = jnp.zeros_like(acc_ref)

    q = (q_ref[...] * ATT_SCALE).astype(BF16)
    zs = []
    for pg in range(pps):
        kb = k_refs[pg][...].reshape(n, HEAD_DIM).astype(BF16)
        zt = lax.dot_general(q, kb, (((1,), (1,)), ((), ())), preferred_element_type=F32)
        zs.append(jnp.sum(jnp.where(own, zt, 0.0), axis=0, keepdims=True))
    z = jnp.concatenate(zs, axis=0) + bias_ref[...]
    sp = _softplus(z)
    later = jnp.where(lane < n - H, pltpu.roll(sp, n - H, 1), 0.0)
    total = sp
    step = H
    while step < n:
        later = later + jnp.where(lane < n - step, pltpu.roll(later, n - step, 1), 0.0)
        total = total + pltpu.roll(total, step, 1)
        step *= 2
    running = carry_ref[...]
    carries = [None] * pps
    for pg in reversed(range(pps)):
        carries[pg] = running
        running = running + total[pg:pg + 1]
    carry_ref[...] = running
    w = jnp.exp(z - sp - later - jnp.concatenate(carries, axis=0))
    acc = acc_ref[...]
    for pg in range(pps):
        vb = v_refs[pg][...].reshape(n, HEAD_DIM).astype(BF16)
        wm = jnp.where(own, jnp.broadcast_to(w[pg:pg + 1], (H, n)), 0.0).astype(BF16)
        acc = acc + jnp.dot(wm, vb, preferred_element_type=F32)
    acc_ref[...] = acc

    @pl.when(j == n_steps - 1)
    def _():
        o_ref[...] = acc


def _sb_sample(q, cache_k, cache_v, page_table, sb_bias, layer, pps=8):
    n_seq, n_pages = page_table.shape
    page = cache_k.shape[2]
    n_steps = n_pages // pps
    H = N_HEADS_ATT

    def cache_spec(pg):
        return pl.BlockSpec(
            (None, None, page, H, HEAD_DIM),
            lambda b, j, pt: (layer, pt[b, n_pages - (j + 1) * pps + pg], 0, 0, 0))

    grid_spec = pltpu.PrefetchScalarGridSpec(
        num_scalar_prefetch=1,
        grid=(n_seq, n_steps),
        in_specs=[
            pl.BlockSpec((1, page * H), lambda b, j, pt: (0, 0)),
            pl.BlockSpec((None, H, HEAD_DIM), lambda b, j, pt: (b, 0, 0)),
        ] + [cache_spec(pg) for pg in range(pps)] * 2,
        out_specs=pl.BlockSpec((None, H, HEAD_DIM), lambda b, j, pt: (b, 0, 0)),
        scratch_shapes=[pltpu.VMEM((1, page * H), F32), pltpu.VMEM((H, HEAD_DIM), F32)],
    )
    bias_row = jnp.tile(sb_bias, page).reshape(1, page * H)
    return pl.pallas_call(
        functools.partial(_sb_sample_kernel, n_steps=n_steps, pps=pps, page=page),
        grid_spec=grid_spec,
        out_shape=jax.ShapeDtypeStruct((n_seq, H, HEAD_DIM), F32),
        compiler_params=_params("parallel", "arbitrary"),
        name="sb_sample",
    )(page_table, bias_row, q, *([cache_k] * pps), *([cache_v] * pps))


def _mix_sample_kernel(p_ref, att_ref, state_ref, wp_ref, scale_ref, g_ref, b_ref, w0_ref, b0_ref,
                       cv_ref, o_ref):
    xa = p_ref[:, 0:D_POOL]
    outs = []
    for g, win in enumerate(POOL_WINDOWS):
        lo, hi = g * POOL_GROUP, (g + 1) * POOL_GROUP
        acc = xa[:, lo:hi]
        for back in range(1, win):
            acc = acc + state_ref[POOL_BUF - back, :, lo:hi]
        diff = acc / float(win) - xa[:, lo:hi]
        outs.append(jnp.dot(diff.astype(BF16), wp_ref[g].astype(BF16), preferred_element_type=F32))
    o_ref[:, 0:D_POOL] = (jnp.concatenate(outs, axis=-1) * scale_ref[...]).astype(o_ref.dtype)
    o_ref[:, D_POOL:D_POOL + D_ATT] = att_ref[...].astype(o_ref.dtype)
    u0 = D_POOL + 3 * D_ATT
    for h in range(N_HEADS_CHUNK):
        lo, hi = h * HEAD_DIM, (h + 1) * HEAD_DIM
        gvn = _layer_norm_rows(_gelu_tanh(p_ref[:, u0 + D_CHUNK + lo:u0 + D_CHUNK + hi]),
                               g_ref[:, lo:hi], b_ref[:, lo:hi])
        cv_ref[:, lo:hi] = gvn
        mixed = w0_ref[:, lo:hi] * gvn + b0_ref[:, lo:hi]
        o_ref[:, D_POOL + D_ATT + lo:D_POOL + D_ATT + hi] = (
            _gelu_tanh(p_ref[:, u0 + lo:u0 + hi]) * mixed).astype(o_ref.dtype)


def _mix_sample(p, att, state, w_pool, pool_scale, ln_v_g, ln_v_b, w_s, b_s):
    S = p.shape[0]
    w0 = jnp.repeat(w_s[:, 0, 0], HEAD_DIM).reshape(1, D_CHUNK)
    b0 = jnp.repeat(b_s[:, 0], HEAD_DIM).reshape(1, D_CHUNK)
    return pl.pallas_call(
        _mix_sample_kernel,
        out_shape=[
            jax.ShapeDtypeStruct((S, D_CHUNK), F32),
            jax.ShapeDtypeStruct((S, D_MODEL), F32),
        ],
        compiler_params=pltpu.CompilerParams(vmem_limit_bytes=VMEM_LIMIT_BYTES),
        name="mix_sample",
    )(p, att, jnp.swapaxes(state, 0, 1), w_pool, pool_scale.reshape(1, D_POOL), ln_v_g.reshape(1, D_CHUNK),
      ln_v_b.reshape(1, D_CHUNK), w0, b0)


def kernel(x_prompt, x_sample, cache_k, cache_v, state_pool, page_table, c_prompt, c_sample,
           ln0_g, ln0_b, w_ada, b_ada, w_in, sb_bias, w_pool, pool_scale, ln_v_g, ln_v_b, w_s, b_s,
           w_out, ln1_g, ln1_b, w_ff1, w_ff2, ln2_g, ln2_b):
    B, T, D = x_prompt.shape
    S = x_sample.shape[0]
    H = N_HEADS_ATT
    wb_in, wb_out = w_in[0].astype(BF16), w_out[0].astype(BF16)
    wb_ff1 = [w_ff1[l].astype(BF16) for l in range(DEPTH)]

    pad = (-(B + S)) % 8
    cs = jnp.concatenate([c_prompt, c_sample, jnp.zeros((pad, D), F32)], axis=0)
    mod = _ada(cs, w_ada, b_ada)
    mod_p = [mod[l, :B].reshape(B, 1, 6 * D) for l in range(DEPTH)]
    mod_s = [mod[l, B:B + S].reshape(1, S, 6 * D) for l in range(DEPTH)]
    xs = x_sample.reshape(1, S, D)

    xp, hp = _ln_mod(x_prompt, ln0_g, ln0_b, mod=mod_p[0], sc_col=1)
    xs, hs = _ln_mod(xs, ln0_g, ln0_b, mod=mod_s[0], sc_col=1, h_dtype=F32)

    k_st = v_st = cv_st = None
    outs = {k: [] for k in ("pp", "ks", "vs", "ps", "cs")}
    for l in range(DEPTH):
        p, ps = _dense(hp.reshape(B * T, D), hs.reshape(S, D), wb_in)
        p = p.reshape(B, T, D_IN)

        a_out = _pool_prompt(p, w_pool[l], pool_scale[l])
        b_out = _sb_prompt(p, sb_bias[l])
        cv_st, c_out = _chunk_prompt(p, ln_v_g[l], ln_v_b[l], w_s[l], b_s[l], l, cv_st)
        k_st, v_st = _emit_kv(p, l, k_st, v_st)
        outs["pp"].append(p[:, T - POOL_BUF:, :D_POOL])
        mix = jnp.concatenate([a_out, b_out, c_out], axis=-1).reshape(B * T, D)

        q_s = ps[:, D_POOL:D_POOL + D_ATT].reshape(S, H, HEAD_DIM)
        att = _sb_sample(q_s, cache_k, cache_v, page_table, sb_bias[l], l).reshape(S, D_ATT)
        cv_s, mix_s = _mix_sample(ps, att, state_pool[l], w_pool[l], pool_scale[l], ln_v_g[l],
                                  ln_v_b[l], w_s[l], b_s[l])
        outs["ks"].append(ps[:, D_POOL + D_ATT:D_POOL + 2 * D_ATT].reshape(S, 1, H, HEAD_DIM))
        outs["vs"].append(ps[:, D_POOL + 2 * D_ATT:D_POOL + 3 * D_ATT].reshape(S, 1, H, HEAD_DIM))
        outs["ps"].append(jnp.concatenate([state_pool[l][:, 1:], ps[:, None, :D_POOL]], axis=1))
        outs["cs"].append(cv_s.reshape(S, 1, N_HEADS_CHUNK, HEAD_DIM))

        mo, mo_s = _dense(mix, mix_s, wb_out)
        xp, h2p = _ln_mod(xp, ln1_g[l], ln1_b[l], y=mo.reshape(B, T, D), mod_res=mod_p[l],
                          gate_col=2, mod=mod_p[l], sc_col=4)
        xs, h2s = _ln_mod(xs, ln1_g[l], ln1_b[l], y=mo_s.reshape(1, S, D), mod_res=mod_s[l],
                          gate_col=2, mod=mod_s[l], sc_col=4, h_dtype=F32)

        nxt = l + 1 < DEPTH
        a, a_s, wb_ff2 = _dense(h2p.reshape(B * T, D), h2s.reshape(S, D), wb_ff1[l],
                                out_dtype=BF16, relu2=True, casts=[(w_ff2, l, 128, D)])
        if nxt:
            f, wb_in, wb_out = _matmul(a, wb_ff2, tm=1024, tn=1024, tk=D,
                                       casts=[(w_in, l + 1, 32, D_IN), (w_out, l + 1, 32, D)])
        else:
            f = _matmul(a, wb_ff2, tm=1024, tn=1024, tk=D)
        f = f.reshape(B, T, D)
        f_s = _matmul(a_s, wb_ff2, tm=S, tn=1024, tk=D).reshape(1, S, D)
        xp, hp = _ln_mod(xp, ln2_g[l], ln2_b[l], y=f, mod_res=mod_p[l], gate_col=5,
                         mod=mod_p[l + 1] if nxt else None, sc_col=1)
        xs, hs = _ln_mod(xs, ln2_g[l], ln2_b[l], y=f_s, mod_res=mod_s[l], gate_col=5,
                         mod=mod_s[l + 1] if nxt else None, sc_col=1, h_dtype=F32)

    st = {k: jnp.stack(v) for k, v in outs.items()}
    return (xp, xs.reshape(S, 1, D), k_st, v_st, st["pp"], cv_st,
            st["ks"], st["vs"], st["ps"], st["cs"])
```

```python
import functools
import math

import jax
import jax.numpy as jnp
from jax import lax
from jax.experimental import pallas as pl
from jax.experimental.pallas import tpu as pltpu

F32 = jnp.float32
BF16 = jnp.bfloat16

D_MODEL = 4096
HEAD_DIM = 128
D_ATT = D_MODEL // 2
D_POOL = D_MODEL // 4
D_CHUNK = D_MODEL // 4
N_HEADS_ATT = D_ATT // HEAD_DIM
N_HEADS_CHUNK = D_CHUNK // HEAD_DIM
POOL_WINDOWS = (2, 4, 8, 16)
POOL_GROUP = D_POOL // len(POOL_WINDOWS)
POOL_BUF = max(POOL_WINDOWS) - 1
POOL_HALO = 16
CHUNK = 128
D_FF = 4 * D_MODEL
D_IN = D_POOL + 3 * D_ATT + 2 * D_CHUNK
DEPTH = 2
ALPHA = (2.0 * DEPTH) ** 0.25
LN_EPS = 1e-5
ATT_SCALE = 1.0 / math.sqrt(HEAD_DIM)
LOG2E = 1.0 / math.log(2.0)

Q_COL128 = D_POOL // HEAD_DIM
K_COL128 = (D_POOL + D_ATT) // HEAD_DIM
V_COL128 = (D_POOL + 2 * D_ATT) // HEAD_DIM
K_COL1024 = (D_POOL + D_ATT) // 1024
V_COL1024 = (D_POOL + 2 * D_ATT) // 1024
U_COL1024 = (D_POOL + 3 * D_ATT) // D_CHUNK
GV_COL1024 = U_COL1024 + 1

VMEM_LIMIT_BYTES = 56 * 1024 * 1024


def _params(*sem):
    return pltpu.CompilerParams(dimension_semantics=sem, vmem_limit_bytes=VMEM_LIMIT_BYTES)


def _gelu_tanh(x):
    c = math.sqrt(2.0 / math.pi)
    return 0.5 * x * (1.0 + jnp.tanh(c * (x + 0.044715 * (x * x * x))))


def _softplus(z):
    return jnp.maximum(z, 0.0) + jnp.log(1.0 + jnp.exp2(jnp.abs(z) * (-LOG2E)))


def _layer_norm_rows(x, g, b):
    mu = jnp.mean(x, axis=-1, keepdims=True)
    xc = x - mu
    var = jnp.mean(xc * xc, axis=-1, keepdims=True)
    return xc * lax.rsqrt(var + LN_EPS) * g + b


def _ada_kernel(c_ref, w_ref, b_ref, o_ref):
    c = c_ref[...]
    s = (c * jax.nn.sigmoid(c)).astype(BF16)
    o_ref[...] = jnp.dot(s, w_ref[...].astype(BF16), preferred_element_type=F32) + b_ref[...]


def _ada(cs, w_ada, b_ada, tn=512):
    rows = cs.shape[0]
    n = w_ada.shape[-1]
    return pl.pallas_call(
        _ada_kernel,
        grid=(DEPTH, n // tn),
        in_specs=[
            pl.BlockSpec((rows, D_MODEL), lambda l, j: (0, 0)),
            pl.BlockSpec((None, D_MODEL, tn), lambda l, j: (l, 0, j)),
            pl.BlockSpec((None, 1, tn), lambda l, j: (l, 0, j)),
        ],
        out_specs=pl.BlockSpec((None, rows, tn), lambda l, j: (l, 0, j)),
        out_shape=jax.ShapeDtypeStruct((DEPTH, rows, n), F32),
        compiler_params=_params("arbitrary", "arbitrary"),
        name="ada_mod",
    )(cs, w_ada, b_ada.reshape(DEPTH, 1, n))


def _ln_mod_kernel(*refs, has_res, has_mod):
    it = iter(refs)
    x_ref = next(it)
    if has_res:
        y_ref, gate_ref = next(it), next(it)
    g_ref, b_ref = next(it), next(it)
    if has_mod:
        sc_ref, sh_ref = next(it), next(it)
    xo_ref = next(it)
    x = x_ref[...]
    if has_res:
        x = ALPHA * x + gate_ref[...] * y_ref[...].astype(F32)
    xn = _layer_norm_rows(x, g_ref[...], b_ref[...])
    xo_ref[...] = xn
    if has_mod:
        ho_ref = next(it)
        ho_ref[...] = (xn * (1.0 + sc_ref[...]) + sh_ref[...]).astype(ho_ref.dtype)


def _ln_mod(x, ln_g, ln_b, *, y=None, mod=None, gate_col=None, mod_res=None, sc_col=None,
            h_dtype=BF16, tt=256):
    B, T, D = x.shape
    tt = min(tt, T)
    has_res = y is not None
    has_mod = mod is not None
    row = pl.BlockSpec((None, tt, D), lambda b, i: (b, i, 0))
    vec = pl.BlockSpec((1, D), lambda b, i: (0, 0))

    def modspec(m, col):
        tm = m.shape[1]
        return pl.BlockSpec((None, tm, D), lambda b, i: (b, 0, col))

    args, specs = [x], [row]
    if has_res:
        args += [y, mod_res]
        specs += [row, modspec(mod_res, gate_col)]
    args += [ln_g.reshape(1, D), ln_b.reshape(1, D)]
    specs += [vec, vec]
    if has_mod:
        args += [mod, mod]
        specs += [modspec(mod, sc_col), modspec(mod, sc_col - 1)]
    out_shape = [jax.ShapeDtypeStruct((B, T, D), F32)]
    out_specs = [row]
    if has_mod:
        out_shape.append(jax.ShapeDtypeStruct((B, T, D), h_dtype))
        out_specs.append(row)
    res = pl.pallas_call(
        functools.partial(_ln_mod_kernel, has_res=has_res, has_mod=has_mod),
        grid=(B, T // tt),
        in_specs=specs,
        out_specs=out_specs,
        out_shape=out_shape,
        compiler_params=_params("parallel", "parallel"),
        name="ln_mod",
    )(*args)
    return (res[0], res[1]) if has_mod else (res[0], None)


def _relu2(acc):
    r = jnp.maximum(acc, 0.0)
    return r * r


def _cast_jobs(jobs, step_of):
    in_specs, out_specs, out_shapes, counts = [], [], [], []
    for src, layer, rb, cb in jobs:
        _, rows, cols = src.shape
        ncb = cols // cb
        count = (rows // rb) * ncb

        def block(*g, count=count, ncb=ncb):
            s = jnp.minimum(step_of(*g), count - 1)
            return s // ncb, s % ncb

        in_specs.append(pl.BlockSpec((None, rb, cb),
                                     lambda *g, block=block, layer=layer: (layer, *block(*g))))
        out_specs.append(pl.BlockSpec((rb, cb), lambda *g, block=block: block(*g)))
        out_shapes.append(jax.ShapeDtypeStruct((rows, cols), BF16))
        counts.append(count)
    return in_specs, out_specs, out_shapes, counts


def _run_casts(step, srcs, dsts, counts):
    for src, dst, count in zip(srcs, dsts, counts):
        @pl.when(step < count)
        def _(src=src, dst=dst):
            dst[...] = src[...].astype(BF16)


def _dense_kernel(x_ref, xs_ref, w_ref, *rest, relu2, counts, ni):
    nc = len(counts)
    srcs, (o_ref, os_ref), dsts = rest[:nc], rest[nc:nc + 2], rest[nc + 2:]
    post = _relu2 if relu2 else (lambda a: a)
    j, i = pl.program_id(0), pl.program_id(1)

    @pl.when(i == 0)
    def _():
        os_ref[...] = post(jnp.dot(xs_ref[...].astype(BF16), w_ref[...],
                                   preferred_element_type=F32)).astype(os_ref.dtype)

    o_ref[...] = post(jnp.dot(x_ref[...], w_ref[...],
                              preferred_element_type=F32)).astype(o_ref.dtype)
    _run_casts(j * ni + i, srcs, dsts, counts)


def _dense(x, xs, w, *, out_dtype=F32, relu2=False, casts=(), tm=1024, tn=1024):
    M, K = x.shape
    S = xs.shape[0]
    N = w.shape[1]
    ni = M // tm
    c_in, c_out, c_shapes, counts = _cast_jobs(casts, lambda j, i: j * ni + i)
    return pl.pallas_call(
        functools.partial(_dense_kernel, relu2=relu2, counts=tuple(counts), ni=ni),
        grid=(N // tn, ni),
        in_specs=[
            pl.BlockSpec((tm, K), lambda j, i: (i, 0)),
            pl.BlockSpec((S, K), lambda j, i: (0, 0)),
            pl.BlockSpec((K, tn), lambda j, i: (0, j)),
        ] + c_in,
        out_specs=[
            pl.BlockSpec((tm, tn), lambda j, i: (i, j)),
            pl.BlockSpec((S, tn), lambda j, i: (0, j)),
        ] + c_out,
        out_shape=[jax.ShapeDtypeStruct((M, N), out_dtype), jax.ShapeDtypeStruct((S, N), F32)] + c_shapes,
        compiler_params=_params("arbitrary", "arbitrary"),
        name="dense",
    )(x, xs, w, *[job[0] for job in casts])


def _mm_kernel(x_ref, w_ref, *rest, nk, nj, relu2, counts):
    nc = len(counts)
    srcs, o_ref, dsts = rest[:nc], rest[nc], rest[nc + 1:2 * nc + 1]
    i, j, k = pl.program_id(0), pl.program_id(1), pl.program_id(2)

    def finish(acc):
        o_ref[...] = (_relu2(acc) if relu2 else acc).astype(o_ref.dtype)

    if nk == 1:
        finish(jnp.dot(x_ref[...].astype(BF16), w_ref[...], preferred_element_type=F32))
    else:
        acc_ref = o_ref if (o_ref.dtype == F32 and not relu2) else rest[-1]

        @pl.when(k == 0)
        def _():
            acc_ref[...] = jnp.zeros_like(acc_ref)

        acc_ref[...] += jnp.dot(x_ref[...].astype(BF16), w_ref[...], preferred_element_type=F32)

        if acc_ref is not o_ref:
            @pl.when(k == nk - 1)
            def _():
                finish(acc_ref[...])

    _run_casts((i * nj + j) * nk + k, srcs, dsts, counts)


def _matmul(x, w, *, tm, tn, tk, out_dtype=F32, relu2=False, casts=()):
    M, K = x.shape
    N = w.shape[1]
    tm, tn, tk = min(tm, M), min(tn, N), min(tk, K)
    nj, nk = N // tn, K // tk
    needs_acc = nk > 1 and (out_dtype != F32 or relu2)
    scratch = [pltpu.VMEM((tm, tn), F32)] if needs_acc else []
    c_in, c_out, c_shapes, counts = _cast_jobs(casts, lambda i, j, k: (i * nj + j) * nk + k)
    res = pl.pallas_call(
        functools.partial(_mm_kernel, nk=nk, nj=nj, relu2=relu2, counts=tuple(counts)),
        grid=(M // tm, nj, nk),
        in_specs=[
            pl.BlockSpec((tm, tk), lambda i, j, k: (i, k)),
            pl.BlockSpec((tk, tn), lambda i, j, k: (k, j)),
        ] + c_in,
        out_specs=[pl.BlockSpec((tm, tn), lambda i, j, k: (i, j))] + c_out,
        out_shape=[jax.ShapeDtypeStruct((M, N), out_dtype)] + c_shapes,
        scratch_shapes=scratch,
        compiler_params=_params("arbitrary", "arbitrary", "arbitrary"),
        name="dense_matmul",
    )(x, w, *[job[0] for job in casts])
    return res if casts else res[0]


def _pool_prompt_kernel(xa_ref, halo_ref, w_ref, scale_ref, o_ref, ext_ref, *, tt):
    i = pl.program_id(1)
    xa = xa_ref[...]
    ext_ref[pl.ds(POOL_HALO, tt), :] = xa

    @pl.when(i == 0)
    def _():
        ext_ref[pl.ds(0, POOL_HALO), :] = jnp.zeros((POOL_HALO, D_POOL), F32)

    @pl.when(i > 0)
    def _():
        ext_ref[pl.ds(0, POOL_HALO), :] = halo_ref[...]

    pos = i * tt + lax.broadcasted_iota(jnp.int32, (tt, 1), 0)
    outs = []
    for g, win in enumerate(POOL_WINDOWS):
        cols = pl.ds(g * POOL_GROUP, POOL_GROUP)
        acc = ext_ref[pl.ds(POOL_HALO, tt), cols]
        for back in range(1, win):
            acc = acc + ext_ref[pl.ds(POOL_HALO - back, tt), cols]
        cnt = jnp.minimum(win, pos + 1).astype(F32)
        diff = acc / cnt - xa[:, g * POOL_GROUP:(g + 1) * POOL_GROUP]
        outs.append(jnp.dot(diff.astype(BF16), w_ref[g].astype(BF16), preferred_element_type=F32))
    y = jnp.concatenate(outs, axis=-1) * scale_ref[...]
    o_ref[...] = y.astype(o_ref.dtype)


def _pool_prompt(p, w_pool, pool_scale, tt=256):
    B, T, _ = p.shape
    halo_blocks = tt // POOL_HALO
    return pl.pallas_call(
        functools.partial(_pool_prompt_kernel, tt=tt),
        grid=(B, T // tt),
        in_specs=[
            pl.BlockSpec((None, tt, D_POOL), lambda b, i: (b, i, 0)),
            pl.BlockSpec((None, POOL_HALO, D_POOL),
                         lambda b, i: (b, jnp.maximum(i * halo_blocks - 1, 0), 0)),
            pl.BlockSpec((len(POOL_WINDOWS), POOL_GROUP, POOL_GROUP), lambda b, i: (0, 0, 0)),
            pl.BlockSpec((1, D_POOL), lambda b, i: (0, 0)),
        ],
        out_specs=pl.BlockSpec((None, tt, D_POOL), lambda b, i: (b, i, 0)),
        out_shape=jax.ShapeDtypeStruct((B, T, D_MODEL), BF16),
        scratch_shapes=[pltpu.VMEM((POOL_HALO + tt, D_POOL), F32)],
        compiler_params=_params("parallel", "arbitrary"),
        name="pool_prompt",
    )(p, p, w_pool, pool_scale.reshape(1, D_POOL))


def _stacked(layer, prev, shape):
    extra_in = [] if prev is None else [prev]
    extra_specs = [] if prev is None else [pl.BlockSpec(memory_space=pl.ANY)]
    return extra_in, extra_specs, jax.ShapeDtypeStruct(shape, F32)


def _chunk_prompt_kernel(u_ref, gv_ref, g_ref, b_ref, ws_ref, bs_ref, *rest):
    cv_ref, o_ref = rest[-2:]
    row = lax.broadcasted_iota(jnp.int32, (CHUNK, CHUNK), 0)
    col = lax.broadcasted_iota(jnp.int32, (CHUNK, CHUNK), 1)
    tril = col <= row
    for h in range(N_HEADS_CHUNK):
        cols = pl.ds(h * HEAD_DIM, HEAD_DIM)
        gvn = _layer_norm_rows(_gelu_tanh(gv_ref[:, cols]), g_ref[:, cols], b_ref[:, cols])
        cv_ref[:, h, :] = gvn
        w = jnp.where(tril, ws_ref[h], 0.0).astype(BF16)
        mixed = jnp.dot(w, gvn.astype(BF16), preferred_element_type=F32) + bs_ref[:, h:h + 1]
        o_ref[:, cols] = (_gelu_tanh(u_ref[:, cols]) * mixed).astype(o_ref.dtype)


def _chunk_prompt(p, ln_v_g, ln_v_b, w_s, b_s, layer, cv_prev, mix):
    B, T, _ = p.shape
    vec = pl.BlockSpec((1, D_CHUNK), lambda b, i: (0, 0))
    extra_in, extra_specs, cv_shape = _stacked(
        layer, cv_prev, (DEPTH, B, T, N_HEADS_CHUNK, HEAD_DIM))
    n_in = 6
    aliases = {n_in: 1} if cv_prev is None else {n_in: 0, n_in + 1: 1}
    extra_in = extra_in + [mix]
    extra_specs = extra_specs + [pl.BlockSpec(memory_space=pl.ANY)]
    return pl.pallas_call(
        _chunk_prompt_kernel,
        grid=(B, T // CHUNK),
        in_specs=[
            pl.BlockSpec((None, CHUNK, D_CHUNK), lambda b, i: (b, i, U_COL1024)),
            pl.BlockSpec((None, CHUNK, D_CHUNK), lambda b, i: (b, i, GV_COL1024)),
            vec, vec,
            pl.BlockSpec((N_HEADS_CHUNK, CHUNK, CHUNK), lambda b, i: (0, 0, 0)),
            pl.BlockSpec((CHUNK, N_HEADS_CHUNK), lambda b, i: (0, 0)),
        ] + extra_specs,
        out_specs=[
            pl.BlockSpec((None, None, CHUNK, N_HEADS_CHUNK, HEAD_DIM),
                         lambda b, i: (layer, b, i, 0, 0)),
            pl.BlockSpec((None, CHUNK, D_CHUNK), lambda b, i: (b, i, (D_POOL + D_ATT) // D_CHUNK)),
        ],
        out_shape=[cv_shape, jax.ShapeDtypeStruct((B, T, D_MODEL), BF16)],
        input_output_aliases=aliases,
        compiler_params=_params("parallel", "parallel"),
        name="chunk_prompt",
    )(p, p, ln_v_g.reshape(1, D_CHUNK), ln_v_b.reshape(1, D_CHUNK), w_s, b_s.T, *extra_in)


def _emit_kv_kernel(k_ref, v_ref, *rest):
    ko_ref, vo_ref = rest[-2:]
    ko_ref[...] = k_ref[...].reshape(ko_ref.shape)
    vo_ref[...] = v_ref[...].reshape(vo_ref.shape)


def _emit_kv(p, layer, k_prev, v_prev, tt=512):
    B, T, _ = p.shape
    half = N_HEADS_ATT // 2
    shape = (DEPTH, B, T, N_HEADS_ATT, HEAD_DIM)
    extra_in = [] if k_prev is None else [k_prev, v_prev]
    any_spec = pl.BlockSpec(memory_space=pl.ANY)
    out_spec = pl.BlockSpec((None, None, tt, half, HEAD_DIM), lambda b, i, s: (layer, b, i, s, 0))
    return pl.pallas_call(
        _emit_kv_kernel,
        grid=(B, T // tt, 2),
        in_specs=[
            pl.BlockSpec((None, tt, half * HEAD_DIM), lambda b, i, s: (b, i, K_COL1024 + s)),
            pl.BlockSpec((None, tt, half * HEAD_DIM), lambda b, i, s: (b, i, V_COL1024 + s)),
        ] + [any_spec] * len(extra_in),
        out_specs=[out_spec, out_spec],
        out_shape=[jax.ShapeDtypeStruct(shape, F32), jax.ShapeDtypeStruct(shape, F32)],
        input_output_aliases={} if k_prev is None else {2: 0, 3: 1},
        compiler_params=_params("parallel", "parallel", "parallel"),
        name="emit_kv",
    )(p, p, *extra_in)


MASKED_LOGIT = -1e30


def _sb_scores(q, kb, bias, mask):
    z = lax.dot_general(q, kb, (((1,), (1,)), ((), ())), preferred_element_type=F32) + bias
    sp = _softplus(z)
    ls = z - sp
    if mask is not None:
        sp = jnp.where(mask, sp, 0.0)
        ls = jnp.where(mask, ls, MASKED_LOGIT)
    return ls, sp.astype(BF16)


def _sb_weights(ls, spb, vb, upper, ones, carry):
    later = jnp.dot(spb, upper, preferred_element_type=F32)
    x = ls - later - jnp.concatenate([carry] * (ls.shape[1] // carry.shape[1]), axis=1)
    pv = jnp.dot(jnp.exp(x).astype(BF16), vb, preferred_element_type=F32)
    del ones
    tot = jnp.broadcast_to(later[:, :1] + spb[:, :1].astype(F32), carry.shape)
    return pv, tot


def _sb_prompt_kernel(bias_ref, q_ref, k_ref, v_ref, mix_ref, *rest, tq, hp, counts):
    del mix_ref
    nc = len(counts)
    srcs, o_ref, dsts = rest[:nc], rest[nc], rest[nc + 1:2 * nc + 1]
    kb_ref, vb_ref, acc_ref, carry_ref, ls_ref, sp_ref = rest[2 * nc + 1:]
    g = pl.program_id(1)
    i = pl.program_id(2)
    step = (pl.program_id(0) * pl.num_programs(1) + g) * pl.num_programs(2) + i
    _run_casts(step, srcs, dsts, counts)

    @pl.when(i == 0)
    def _():
        kb_ref[...] = k_ref[...].astype(BF16)
        vb_ref[...] = v_ref[...].astype(BF16)

    row = lax.broadcasted_iota(jnp.int32, (tq, tq), 0)
    col = lax.broadcasted_iota(jnp.int32, (tq, tq), 1)
    upper = (row > col).astype(BF16)
    ones = jnp.ones((tq, HEAD_DIM), BF16)
    qs = [(q_ref[:, h * HEAD_DIM:(h + 1) * HEAD_DIM] * ATT_SCALE).astype(BF16) for h in range(hp)]
    biases = [bias_ref[g * hp + h] for h in range(hp)]

    def key_rows(m):
        return pl.ds(pl.multiple_of((i - m) * tq, tq), tq)

    def stage1(m, mask):
        rows, slot = key_rows(m), m % 2
        for h in range(hp):
            ls, spb = _sb_scores(qs[h], kb_ref[rows, pl.ds(h * HEAD_DIM, HEAD_DIM)], biases[h], mask)
            ls_ref[slot, h] = ls
            sp_ref[slot, h] = spb

    def stage2(m):
        rows, slot = key_rows(m), m % 2
        for h in range(hp):
            pv, tot = _sb_weights(ls_ref[slot, h], sp_ref[slot, h],
                                  vb_ref[rows, pl.ds(h * HEAD_DIM, HEAD_DIM)], upper, ones, carry_ref[h])
            acc_ref[h] += pv
            carry_ref[h] += tot

    acc_ref[...] = jnp.zeros_like(acc_ref)
    carry_ref[...] = jnp.zeros_like(carry_ref)
    stage1(0, col < row)

    def body(m, c):
        stage2(m - 1)
        stage1(m, None)
        return c

    lax.fori_loop(1, i + 1, body, 0)
    stage2(i)
    for h in range(hp):
        o_ref[:, h * HEAD_DIM:(h + 1) * HEAD_DIM] = acc_ref[h].astype(o_ref.dtype)


def _sb_prompt(p, sb_bias, mix, casts=(), tq=256, hp=4):
    B, T, _ = p.shape
    w = hp * HEAD_DIM
    ng, ni = N_HEADS_ATT // hp, T // tq
    c_in, c_out, c_shapes, counts = _cast_jobs(casts, lambda b, g, i: (b * ng + g) * ni + i)
    return pl.pallas_call(
        functools.partial(_sb_prompt_kernel, tq=tq, hp=hp, counts=tuple(counts)),
        grid=(B, ng, ni),
        in_specs=[
            pl.BlockSpec(memory_space=pltpu.SMEM),
            pl.BlockSpec((None, tq, w), lambda b, g, i: (b, i, Q_COL128 // hp + g)),
            pl.BlockSpec((None, T, w), lambda b, g, i: (b, 0, K_COL128 // hp + g)),
            pl.BlockSpec((None, T, w), lambda b, g, i: (b, 0, V_COL128 // hp + g)),
            pl.BlockSpec(memory_space=pl.ANY),
        ] + c_in,
        out_specs=[pl.BlockSpec((None, tq, w), lambda b, g, i: (b, i, D_POOL // w + g))] + c_out,
        out_shape=[jax.ShapeDtypeStruct((B, T, D_MODEL), BF16)] + c_shapes,
        input_output_aliases={4: 0},
        scratch_shapes=[
            pltpu.VMEM((T, w), BF16), pltpu.VMEM((T, w), BF16),
            pltpu.VMEM((hp, tq, HEAD_DIM), F32), pltpu.VMEM((hp, tq, HEAD_DIM), F32),
            pltpu.VMEM((2, hp, tq, tq), F32), pltpu.VMEM((2, hp, tq, tq), BF16),
        ],
        compiler_params=_params("arbitrary", "arbitrary", "arbitrary"),
        name="sb_prompt",
    )(sb_bias, p, p, p, mix, *[job[0] for job in casts])


def _sb_sample_kernel(pt_ref, bias_ref, q_ref, *refs, n_steps, pps, page):
    del pt_ref
    k_refs, v_refs = refs[:pps], refs[pps:2 * pps]
    o_ref, carry_ref, acc_ref = refs[2 * pps:]
    j = pl.program_id(1)
    H = N_HEADS_ATT
    n = page * H
    lane = lax.broadcasted_iota(jnp.int32, (pps, n), 1)
    own = (lax.broadcasted_iota(jnp.int32, (H, n), 1) % H) == lax.broadcasted_iota(jnp.int32, (H, n), 0)

    @pl.when(j == 0)
    def _():
        carry_ref[...] = jnp.zeros_like(carry_ref)
        acc_ref[...] = jnp.zeros_like(acc_ref)

    q = (q_ref[...] * ATT_SCALE).astype(BF16)
    zs = []
    for pg in range(pps):
        kb = k_refs[pg][...].reshape(n, HEAD_DIM).astype(BF16)
        zt = lax.dot_general(q, kb, (((1,), (1,)), ((), ())), preferred_element_type=F32)
        zs.append(jnp.sum(jnp.where(own, zt, 0.0), axis=0, keepdims=True))
    z = jnp.concatenate(zs, axis=0) + bias_ref[...]
    sp = _softplus(z)
    later = jnp.where(lane < n - H, pltpu.roll(sp, n - H, 1), 0.0)
    total = sp
    step = H
    while step < n:
        later = later + jnp.where(lane < n - step, pltpu.roll(later, n - step, 1), 0.0)
        total = total + pltpu.roll(total, step, 1)
        step *= 2
    running = carry_ref[...]
    carries = [None] * pps
    for pg in reversed(range(pps)):
        carries[pg] = running
        running = running + total[pg:pg + 1]
    carry_ref[...] = running
    w = jnp.exp(z - sp - later - jnp.concatenate(carries, axis=0))
    acc = acc_ref[...]
    for pg in range(pps):
        vb = v_refs[pg][...].reshape(n, HEAD_DIM).astype(BF16)
        wm = jnp.where(own, jnp.broadcast_to(w[pg:pg + 1], (H, n)), 0.0).astype(BF16)
        acc = acc + jnp.dot(wm, vb, preferred_element_type=F32)
    acc_ref[...] = acc

    @pl.when(j == n_steps - 1)
    def _():
        o_ref[...] = acc


def _sb_sample(q, cache_k, cache_v, page_table, sb_bias, layer, pps=8):
    n_seq, n_pages = page_table.shape
    page = cache_k.shape[2]
    n_steps = n_pages // pps
    H = N_HEADS_ATT

    def cache_spec(pg):
        return pl.BlockSpec(
            (None, None, page, H, HEAD_DIM),
            lambda b, j, pt: (layer, pt[b, n_pages - (j + 1) * pps + pg], 0, 0, 0))

    grid_spec = pltpu.PrefetchScalarGridSpec(
        num_scalar_prefetch=1,
        grid=(n_seq, n_steps),
        in_specs=[
            pl.BlockSpec((1, page * H), lambda b, j, pt: (0, 0)),
            pl.BlockSpec((None, H, HEAD_DIM), lambda b, j, pt: (b, 0, 0)),
        ] + [cache_spec(pg) for pg in range(pps)] * 2,
        out_specs=pl.BlockSpec((None, H, HEAD_DIM), lambda b, j, pt: (b, 0, 0)),
        scratch_shapes=[pltpu.VMEM((1, page * H), F32), pltpu.VMEM((H, HEAD_DIM), F32)],
    )
    bias_row = jnp.tile(sb_bias, page).reshape(1, page * H)
    return pl.pallas_call(
        functools.partial(_sb_sample_kernel, n_steps=n_steps, pps=pps, page=page),
        grid_spec=grid_spec,
        out_shape=jax.ShapeDtypeStruct((n_seq, H, HEAD_DIM), F32),
        compiler_params=_params("parallel", "arbitrary"),
        name="sb_sample",
    )(page_table, bias_row, q, *([cache_k] * pps), *([cache_v] * pps))


def _mix_sample_kernel(p_ref, att_ref, state_ref, wp_ref, scale_ref, g_ref, b_ref, w0_ref, b0_ref,
                       cv_ref, o_ref):
    xa = p_ref[:, 0:D_POOL]
    outs = []
    for g, win in enumerate(POOL_WINDOWS):
        lo, hi = g * POOL_GROUP, (g + 1) * POOL_GROUP
        acc = xa[:, lo:hi]
        for back in range(1, win):
            acc = acc + state_ref[POOL_BUF - back, :, lo:hi]
        diff = acc / float(win) - xa[:, lo:hi]
        outs.append(jnp.dot(diff.astype(BF16), wp_ref[g].astype(BF16), preferred_element_type=F32))
    o_ref[:, 0:D_POOL] = (jnp.concatenate(outs, axis=-1) * scale_ref[...]).astype(o_ref.dtype)
    o_ref[:, D_POOL:D_POOL + D_ATT] = att_ref[...].astype(o_ref.dtype)
    u0 = D_POOL + 3 * D_ATT
    for h in range(N_HEADS_CHUNK):
        lo, hi = h * HEAD_DIM, (h + 1) * HEAD_DIM
        gvn = _layer_norm_rows(_gelu_tanh(p_ref[:, u0 + D_CHUNK + lo:u0 + D_CHUNK + hi]),
                               g_ref[:, lo:hi], b_ref[:, lo:hi])
        cv_ref[:, lo:hi] = gvn
        mixed = w0_ref[:, lo:hi] * gvn + b0_ref[:, lo:hi]
        o_ref[:, D_POOL + D_ATT + lo:D_POOL + D_ATT + hi] = (
            _gelu_tanh(p_ref[:, u0 + lo:u0 + hi]) * mixed).astype(o_ref.dtype)


def _mix_sample(p, att, state, w_pool, pool_scale, ln_v_g, ln_v_b, w_s, b_s):
    S = p.shape[0]
    w0 = jnp.repeat(w_s[:, 0, 0], HEAD_DIM).reshape(1, D_CHUNK)
    b0 = jnp.repeat(b_s[:, 0], HEAD_DIM).reshape(1, D_CHUNK)
    return pl.pallas_call(
        _mix_sample_kernel,
        out_shape=[
            jax.ShapeDtypeStruct((S, D_CHUNK), F32),
            jax.ShapeDtypeStruct((S, D_MODEL), F32),
        ],
        compiler_params=pltpu.CompilerParams(vmem_limit_bytes=VMEM_LIMIT_BYTES),
        name="mix_sample",
    )(p, att, jnp.swapaxes(state, 0, 1), w_pool, pool_scale.reshape(1, D_POOL), ln_v_g.reshape(1, D_CHUNK),
      ln_v_b.reshape(1, D_CHUNK), w0, b0)


def kernel(x_prompt, x_sample, cache_k, cache_v, state_pool, page_table, c_prompt, c_sample,
           ln0_g, ln0_b, w_ada, b_ada, w_in, sb_bias, w_pool, pool_scale, ln_v_g, ln_v_b, w_s, b_s,
           w_out, ln1_g, ln1_b, w_ff1, w_ff2, ln2_g, ln2_b):
    B, T, D = x_prompt.shape
    S = x_sample.shape[0]
    H = N_HEADS_ATT
    wb_in = w_in[0].astype(BF16)
    wb_out = None

    pad = (-(B + S)) % 8
    cs = jnp.concatenate([c_prompt, c_sample, jnp.zeros((pad, D), F32)], axis=0)
    mod = _ada(cs, w_ada, b_ada)
    mod_p = [mod[l, :B].reshape(B, 1, 6 * D) for l in range(DEPTH)]
    mod_s = [mod[l, B:B + S].reshape(1, S, 6 * D) for l in range(DEPTH)]
    xs = x_sample.reshape(1, S, D)

    xp, hp = _ln_mod(x_prompt, ln0_g, ln0_b, mod=mod_p[0], sc_col=1)
    xs, hs = _ln_mod(xs, ln0_g, ln0_b, mod=mod_s[0], sc_col=1, h_dtype=F32)

    k_st = v_st = cv_st = None
    outs = {k: [] for k in ("pp", "ks", "vs", "ps", "cs")}
    for l in range(DEPTH):
        if wb_out is None:
            p, ps, wb_out = _dense(hp.reshape(B * T, D), hs.reshape(S, D), wb_in,
                                   casts=[(w_out, l, 64, D)])
        else:
            p, ps = _dense(hp.reshape(B * T, D), hs.reshape(S, D), wb_in)
        p = p.reshape(B, T, D_IN)

        mix = _pool_prompt(p, w_pool[l], pool_scale[l])
        mix, wb_ff1 = _sb_prompt(p, sb_bias[l], mix, casts=[(w_ff1, l, 32, D_FF)])
        cv_st, mix = _chunk_prompt(p, ln_v_g[l], ln_v_b[l], w_s[l], b_s[l], l, cv_st, mix)
        k_st, v_st = _emit_kv(p, l, k_st, v_st)
        outs["pp"].append(p[:, T - POOL_BUF:, :D_POOL])
        mix = mix.reshape(B * T, D)

        q_s = ps[:, D_POOL:D_POOL + D_ATT].reshape(S, H, HEAD_DIM)
        att = _sb_sample(q_s, cache_k, cache_v, page_table, sb_bias[l], l).reshape(S, D_ATT)
        cv_s, mix_s = _mix_sample(ps, att, state_pool[l], w_pool[l], pool_scale[l], ln_v_g[l],
                                  ln_v_b[l], w_s[l], b_s[l])
        outs["ks"].append(ps[:, D_POOL + D_ATT:D_POOL + 2 * D_ATT].reshape(S, 1, H, HEAD_DIM))
        outs["vs"].append(ps[:, D_POOL + 2 * D_ATT:D_POOL + 3 * D_ATT].reshape(S, 1, H, HEAD_DIM))
        outs["ps"].append(jnp.concatenate([state_pool[l][:, 1:], ps[:, None, :D_POOL]], axis=1))
        outs["cs"].append(cv_s.reshape(S, 1, N_HEADS_CHUNK, HEAD_DIM))

        mo, mo_s = _dense(mix, mix_s, wb_out)
        xp, h2p = _ln_mod(xp, ln1_g[l], ln1_b[l], y=mo.reshape(B, T, D), mod_res=mod_p[l],
                          gate_col=2, mod=mod_p[l], sc_col=4)
        xs, h2s = _ln_mod(xs, ln1_g[l], ln1_b[l], y=mo_s.reshape(1, S, D), mod_res=mod_s[l],
                          gate_col=2, mod=mod_s[l], sc_col=4, h_dtype=F32)

        nxt = l + 1 < DEPTH
        a, a_s, wb_ff2 = _dense(h2p.reshape(B * T, D), h2s.reshape(S, D), wb_ff1,
                                out_dtype=BF16, relu2=True, casts=[(w_ff2, l, 128, D)])
        if nxt:
            f, wb_in, wb_out = _matmul(a, wb_ff2, tm=1024, tn=1024, tk=D,
                                       casts=[(w_in, l + 1, 32, D_IN), (w_out, l + 1, 32, D)])
        else:
            f = _matmul(a, wb_ff2, tm=1024, tn=1024, tk=D)
        f = f.reshape(B, T, D)
        f_s = _matmul(a_s, wb_ff2, tm=S, tn=1024, tk=D).reshape(1, S, D)
        xp, hp = _ln_mod(xp, ln2_g[l], ln2_b[l], y=f, mod_res=mod_p[l], gate_col=5,
                         mod=mod_p[l + 1] if nxt else None, sc_col=1)
        xs, hs = _ln_mod(xs, ln2_g[l], ln2_b[l], y=f_s, mod_res=mod_s[l], gate_col=5,
                         mod=mod_s[l + 1] if nxt else None, sc_col=1, h_dtype=F32)

    st = {k: jnp.stack(v) for k, v in outs.items()}
    return (xp, xs.reshape(S, 1, D), k_st, v_st, st["pp"], cv_st,
            st["ks"], st["vs"], st["ps"], st["cs"])
```

```python
import functools
import math

import jax
import jax.numpy as jnp
from jax import lax
from jax.experimental import pallas as pl
from jax.experimental.pallas import tpu as pltpu

F32 = jnp.float32
BF16 = jnp.bfloat16

D_MODEL = 4096
HEAD_DIM = 128
D_ATT = D_MODEL // 2
D_POOL = D_MODEL // 4
D_CHUNK = D_MODEL // 4
N_HEADS_ATT = D_ATT // HEAD_DIM
N_HEADS_CHUNK = D_CHUNK // HEAD_DIM
POOL_WINDOWS = (2, 4, 8, 16)
POOL_GROUP = D_POOL // len(POOL_WINDOWS)
POOL_BUF = max(POOL_WINDOWS) - 1
POOL_HALO = 16
CHUNK = 128
D_FF = 4 * D_MODEL
D_IN = D_POOL + 3 * D_ATT + 2 * D_CHUNK
DEPTH = 2
ALPHA = (2.0 * DEPTH) ** 0.25
LN_EPS = 1e-5
ATT_SCALE = 1.0 / math.sqrt(HEAD_DIM)
LOG2E = 1.0 / math.log(2.0)

Q_COL128 = D_POOL // HEAD_DIM
K_COL128 = (D_POOL + D_ATT) // HEAD_DIM
V_COL128 = (D_POOL + 2 * D_ATT) // HEAD_DIM
K_COL1024 = (D_POOL + D_ATT) // 1024
V_COL1024 = (D_POOL + 2 * D_ATT) // 1024
U_COL1024 = (D_POOL + 3 * D_ATT) // D_CHUNK
GV_COL1024 = U_COL1024 + 1

VMEM_LIMIT_BYTES = 56 * 1024 * 1024


def _params(*sem):
    return pltpu.CompilerParams(dimension_semantics=sem, vmem_limit_bytes=VMEM_LIMIT_BYTES)


def _gelu_tanh(x):
    c = math.sqrt(2.0 / math.pi)
    return 0.5 * x * (1.0 + jnp.tanh(c * (x + 0.044715 * (x * x * x))))


def _softplus(z):
    return jnp.maximum(z, 0.0) + jnp.log(1.0 + jnp.exp2(jnp.abs(z) * (-LOG2E)))


def _layer_norm_rows(x, g, b):
    mu = jnp.mean(x, axis=-1, keepdims=True)
    xc = x - mu
    var = jnp.mean(xc * xc, axis=-1, keepdims=True)
    return xc * lax.rsqrt(var + LN_EPS) * g + b


def _ada_kernel(c_ref, w_ref, b_ref, o_ref):
    c = c_ref[...]
    s = (c * jax.nn.sigmoid(c)).astype(BF16)
    o_ref[...] = jnp.dot(s, w_ref[...].astype(BF16), preferred_element_type=F32) + b_ref[...]


def _ada(cs, w_ada, b_ada, tn=512):
    rows = cs.shape[0]
    n = w_ada.shape[-1]
    return pl.pallas_call(
        _ada_kernel,
        grid=(DEPTH, n // tn),
        in_specs=[
            pl.BlockSpec((rows, D_MODEL), lambda l, j: (0, 0)),
            pl.BlockSpec((None, D_MODEL, tn), lambda l, j: (l, 0, j)),
            pl.BlockSpec((None, 1, tn), lambda l, j: (l, 0, j)),
        ],
        out_specs=pl.BlockSpec((None, rows, tn), lambda l, j: (l, 0, j)),
        out_shape=jax.ShapeDtypeStruct((DEPTH, rows, n), F32),
        compiler_params=_params("arbitrary", "arbitrary"),
        name="ada_mod",
    )(cs, w_ada, b_ada.reshape(DEPTH, 1, n))


def _ln_mod_kernel(*refs, has_res, has_mod):
    it = iter(refs)
    x_ref = next(it)
    if has_res:
        y_ref, gate_ref = next(it), next(it)
    g_ref, b_ref = next(it), next(it)
    if has_mod:
        sc_ref, sh_ref = next(it), next(it)
    xo_ref = next(it)
    x = x_ref[...]
    if has_res:
        x = ALPHA * x + gate_ref[...] * y_ref[...].astype(F32)
    xn = _layer_norm_rows(x, g_ref[...], b_ref[...])
    xo_ref[...] = xn
    if has_mod:
        ho_ref = next(it)
        ho_ref[...] = (xn * (1.0 + sc_ref[...]) + sh_ref[...]).astype(ho_ref.dtype)


def _ln_mod(x, ln_g, ln_b, *, y=None, mod=None, gate_col=None, mod_res=None, sc_col=None,
            h_dtype=BF16, tt=256):
    B, T, D = x.shape
    tt = min(tt, T)
    has_res = y is not None
    has_mod = mod is not None
    row = pl.BlockSpec((None, tt, D), lambda b, i: (b, i, 0))
    vec = pl.BlockSpec((1, D), lambda b, i: (0, 0))

    def modspec(m, col):
        tm = m.shape[1]
        return pl.BlockSpec((None, tm, D), lambda b, i: (b, 0, col))

    args, specs = [x], [row]
    if has_res:
        args += [y, mod_res]
        specs += [row, modspec(mod_res, gate_col)]
    args += [ln_g.reshape(1, D), ln_b.reshape(1, D)]
    specs += [vec, vec]
    if has_mod:
        args += [mod, mod]
        specs += [modspec(mod, sc_col), modspec(mod, sc_col - 1)]
    out_shape = [jax.ShapeDtypeStruct((B, T, D), F32)]
    out_specs = [row]
    if has_mod:
        out_shape.append(jax.ShapeDtypeStruct((B, T, D), h_dtype))
        out_specs.append(row)
    res = pl.pallas_call(
        functools.partial(_ln_mod_kernel, has_res=has_res, has_mod=has_mod),
        grid=(B, T // tt),
        in_specs=specs,
        out_specs=out_specs,
        out_shape=out_shape,
        compiler_params=_params("parallel", "parallel"),
        name="ln_mod",
    )(*args)
    return (res[0], res[1]) if has_mod else (res[0], None)


def _relu2(acc):
    r = jnp.maximum(acc, 0.0)
    return r * r


def _cast_jobs(jobs, step_of):
    in_specs, out_specs, out_shapes, counts = [], [], [], []
    for job in jobs:
        if len(job) == 5:
            src, col, layer, rb, _ = job
            rows = src.shape[0]
            count = 2 * (rows // rb)

            def half(*g, count=count):
                s = jnp.minimum(step_of(*g), count - 1)
                return s // 2, s % 2

            in_specs.append(pl.BlockSpec(
                (rb, 1024), lambda *g, half=half, col=col: (half(*g)[0], col + half(*g)[1])))
            out_specs.append(pl.BlockSpec(
                (None, rb, N_HEADS_ATT // 2, HEAD_DIM),
                lambda *g, half=half, layer=layer: (layer, *half(*g), 0)))
            out_shapes.append(jax.ShapeDtypeStruct((DEPTH, rows, N_HEADS_ATT, HEAD_DIM), F32))
            counts.append(count)
            continue
        src, layer, rb, cb = job
        _, rows, cols = src.shape
        ncb = cols // cb
        count = (rows // rb) * ncb

        def block(*g, count=count, ncb=ncb):
            s = jnp.minimum(step_of(*g), count - 1)
            return s // ncb, s % ncb

        in_specs.append(pl.BlockSpec((None, rb, cb),
                                     lambda *g, block=block, layer=layer: (layer, *block(*g))))
        out_specs.append(pl.BlockSpec((rb, cb), lambda *g, block=block: block(*g)))
        out_shapes.append(jax.ShapeDtypeStruct((rows, cols), BF16))
        counts.append(count)
    return in_specs, out_specs, out_shapes, counts


def _run_casts(step, srcs, dsts, counts):
    for src, dst, count in zip(srcs, dsts, counts):
        @pl.when(step < count)
        def _(src=src, dst=dst):
            dst[...] = src[...].astype(dst.dtype).reshape(dst.shape)


def _dense_kernel(x_ref, xs_ref, w_ref, *rest, relu2, counts, ni, n_prev):
    nc = len(counts)
    srcs, rest = rest[:nc], rest[nc + n_prev:]
    (o_ref, os_ref), dsts = rest[:2], rest[2:]
    post = _relu2 if relu2 else (lambda a: a)
    j, i = pl.program_id(0), pl.program_id(1)

    @pl.when(i == 0)
    def _():
        os_ref[...] = post(jnp.dot(xs_ref[...].astype(BF16), w_ref[...],
                                   preferred_element_type=F32)).astype(os_ref.dtype)

    o_ref[...] = post(jnp.dot(x_ref[...], w_ref[...],
                              preferred_element_type=F32)).astype(o_ref.dtype)
    _run_casts(j * ni + i, srcs, dsts, counts)


def _dense(x, xs, w, *, out_dtype=F32, relu2=False, casts=(), tm=1024, tn=1024):
    M, K = x.shape
    S = xs.shape[0]
    N = w.shape[1]
    ni = M // tm
    c_in, c_out, c_shapes, counts = _cast_jobs(casts, lambda j, i: j * ni + i)
    prevs = [(n, job[4]) for n, job in enumerate(casts) if len(job) == 5 and job[4] is not None]
    n_fixed_in, n_fixed_out = 3, 2
    aliases = {n_fixed_in + len(casts) + k: n_fixed_out + n for k, (n, _) in enumerate(prevs)}
    return pl.pallas_call(
        functools.partial(_dense_kernel, relu2=relu2, counts=tuple(counts), ni=ni, n_prev=len(prevs)),
        grid=(N // tn, ni),
        in_specs=[
            pl.BlockSpec((tm, K), lambda j, i: (i, 0)),
            pl.BlockSpec((S, K), lambda j, i: (0, 0)),
            pl.BlockSpec((K, tn), lambda j, i: (0, j)),
        ] + c_in + [pl.BlockSpec(memory_space=pl.ANY)] * len(prevs),
        out_specs=[
            pl.BlockSpec((tm, tn), lambda j, i: (i, j)),
            pl.BlockSpec((S, tn), lambda j, i: (0, j)),
        ] + c_out,
        out_shape=[jax.ShapeDtypeStruct((M, N), out_dtype), jax.ShapeDtypeStruct((S, N), F32)] + c_shapes,
        input_output_aliases=aliases,
        compiler_params=_params("arbitrary", "arbitrary"),
        name="dense",
    )(x, xs, w, *[job[0] for job in casts], *[prev for _, prev in prevs])


def _mm_kernel(x_ref, w_ref, *rest, nk, nj, relu2, counts):
    nc = len(counts)
    srcs, o_ref, dsts = rest[:nc], rest[nc], rest[nc + 1:2 * nc + 1]
    i, j, k = pl.program_id(0), pl.program_id(1), pl.program_id(2)

    def finish(acc):
        o_ref[...] = (_relu2(acc) if relu2 else acc).astype(o_ref.dtype)

    if nk == 1:
        finish(jnp.dot(x_ref[...].astype(BF16), w_ref[...], preferred_element_type=F32))
    else:
        acc_ref = o_ref if (o_ref.dtype == F32 and not relu2) else rest[-1]

        @pl.when(k == 0)
        def _():
            acc_ref[...] = jnp.zeros_like(acc_ref)

        acc_ref[...] += jnp.dot(x_ref[...].astype(BF16), w_ref[...], preferred_element_type=F32)

        if acc_ref is not o_ref:
            @pl.when(k == nk - 1)
            def _():
                finish(acc_ref[...])

    _run_casts((i * nj + j) * nk + k, srcs, dsts, counts)


def _matmul(x, w, *, tm, tn, tk, out_dtype=F32, relu2=False, casts=()):
    M, K = x.shape
    N = w.shape[1]
    tm, tn, tk = min(tm, M), min(tn, N), min(tk, K)
    nj, nk = N // tn, K // tk
    needs_acc = nk > 1 and (out_dtype != F32 or relu2)
    scratch = [pltpu.VMEM((tm, tn), F32)] if needs_acc else []
    c_in, c_out, c_shapes, counts = _cast_jobs(casts, lambda i, j, k: (i * nj + j) * nk + k)
    res = pl.pallas_call(
        functools.partial(_mm_kernel, nk=nk, nj=nj, relu2=relu2, counts=tuple(counts)),
        grid=(M // tm, nj, nk),
        in_specs=[
            pl.BlockSpec((tm, tk), lambda i, j, k: (i, k)),
            pl.BlockSpec((tk, tn), lambda i, j, k: (k, j)),
        ] + c_in,
        out_specs=[pl.BlockSpec((tm, tn), lambda i, j, k: (i, j))] + c_out,
        out_shape=[jax.ShapeDtypeStruct((M, N), out_dtype)] + c_shapes,
        scratch_shapes=scratch,
        compiler_params=_params("arbitrary", "arbitrary", "arbitrary"),
        name="dense_matmul",
    )(x, w, *[job[0] for job in casts])
    return res if casts else res[0]


def _pool_prompt_kernel(xa_ref, halo_ref, w_ref, scale_ref, o_ref, ext_ref, *, tt):
    i = pl.program_id(1)
    xa = xa_ref[...]
    ext_ref[pl.ds(POOL_HALO, tt), :] = xa

    @pl.when(i == 0)
    def _():
        ext_ref[pl.ds(0, POOL_HALO), :] = jnp.zeros((POOL_HALO, D_POOL), F32)

    @pl.when(i > 0)
    def _():
        ext_ref[pl.ds(0, POOL_HALO), :] = halo_ref[...]

    pos = i * tt + lax.broadcasted_iota(jnp.int32, (tt, 1), 0)
    outs = []
    for g, win in enumerate(POOL_WINDOWS):
        cols = pl.ds(g * POOL_GROUP, POOL_GROUP)
        acc = ext_ref[pl.ds(POOL_HALO, tt), cols]
        for back in range(1, win):
            acc = acc + ext_ref[pl.ds(POOL_HALO - back, tt), cols]
        cnt = jnp.minimum(win, pos + 1).astype(F32)
        diff = acc / cnt - xa[:, g * POOL_GROUP:(g + 1) * POOL_GROUP]
        outs.append(jnp.dot(diff.astype(BF16), w_ref[g].astype(BF16), preferred_element_type=F32))
    y = jnp.concatenate(outs, axis=-1) * scale_ref[...]
    o_ref[...] = y.astype(o_ref.dtype)


def _pool_prompt(p, w_pool, pool_scale, tt=256):
    B, T, _ = p.shape
    halo_blocks = tt // POOL_HALO
    return pl.pallas_call(
        functools.partial(_pool_prompt_kernel, tt=tt),
        grid=(B, T // tt),
        in_specs=[
            pl.BlockSpec((None, tt, D_POOL), lambda b, i: (b, i, 0)),
            pl.BlockSpec((None, POOL_HALO, D_POOL),
                         lambda b, i: (b, jnp.maximum(i * halo_blocks - 1, 0), 0)),
            pl.BlockSpec((len(POOL_WINDOWS), POOL_GROUP, POOL_GROUP), lambda b, i: (0, 0, 0)),
            pl.BlockSpec((1, D_POOL), lambda b, i: (0, 0)),
        ],
        out_specs=pl.BlockSpec((None, tt, D_POOL), lambda b, i: (b, i, 0)),
        out_shape=jax.ShapeDtypeStruct((B, T, D_MODEL), BF16),
        scratch_shapes=[pltpu.VMEM((POOL_HALO + tt, D_POOL), F32)],
        compiler_params=_params("parallel", "arbitrary"),
        name="pool_prompt",
    )(p, p, w_pool, pool_scale.reshape(1, D_POOL))


def _stacked(layer, prev, shape):
    extra_in = [] if prev is None else [prev]
    extra_specs = [] if prev is None else [pl.BlockSpec(memory_space=pl.ANY)]
    return extra_in, extra_specs, jax.ShapeDtypeStruct(shape, F32)


def _chunk_prompt_kernel(u_ref, gv_ref, g_ref, b_ref, ws_ref, bs_ref, *rest):
    cv_ref, o_ref = rest[-2:]
    row = lax.broadcasted_iota(jnp.int32, (CHUNK, CHUNK), 0)
    col = lax.broadcasted_iota(jnp.int32, (CHUNK, CHUNK), 1)
    tril = col <= row
    for h in range(N_HEADS_CHUNK):
        cols = pl.ds(h * HEAD_DIM, HEAD_DIM)
        gvn = _layer_norm_rows(_gelu_tanh(gv_ref[:, cols]), g_ref[:, cols], b_ref[:, cols])
        cv_ref[:, h, :] = gvn
        w = jnp.where(tril, ws_ref[h], 0.0).astype(BF16)
        mixed = jnp.dot(w, gvn.astype(BF16), preferred_element_type=F32) + bs_ref[:, h:h + 1]
        o_ref[:, cols] = (_gelu_tanh(u_ref[:, cols]) * mixed).astype(o_ref.dtype)


def _chunk_prompt(p, ln_v_g, ln_v_b, w_s, b_s, layer, cv_prev, mix):
    B, T, _ = p.shape
    vec = pl.BlockSpec((1, D_CHUNK), lambda b, i: (0, 0))
    extra_in, extra_specs, cv_shape = _stacked(
        layer, cv_prev, (DEPTH, B, T, N_HEADS_CHUNK, HEAD_DIM))
    n_in = 6
    aliases = {n_in: 1} if cv_prev is None else {n_in: 0, n_in + 1: 1}
    extra_in = extra_in + [mix]
    extra_specs = extra_specs + [pl.BlockSpec(memory_space=pl.ANY)]
    return pl.pallas_call(
        _chunk_prompt_kernel,
        grid=(B, T // CHUNK),
        in_specs=[
            pl.BlockSpec((None, CHUNK, D_CHUNK), lambda b, i: (b, i, U_COL1024)),
            pl.BlockSpec((None, CHUNK, D_CHUNK), lambda b, i: (b, i, GV_COL1024)),
            vec, vec,
            pl.BlockSpec((N_HEADS_CHUNK, CHUNK, CHUNK), lambda b, i: (0, 0, 0)),
            pl.BlockSpec((CHUNK, N_HEADS_CHUNK), lambda b, i: (0, 0)),
        ] + extra_specs,
        out_specs=[
            pl.BlockSpec((None, None, CHUNK, N_HEADS_CHUNK, HEAD_DIM),
                         lambda b, i: (layer, b, i, 0, 0)),
            pl.BlockSpec((None, CHUNK, D_CHUNK), lambda b, i: (b, i, (D_POOL + D_ATT) // D_CHUNK)),
        ],
        out_shape=[cv_shape, jax.ShapeDtypeStruct((B, T, D_MODEL), BF16)],
        input_output_aliases=aliases,
        compiler_params=_params("parallel", "parallel"),
        name="chunk_prompt",
    )(p, p, ln_v_g.reshape(1, D_CHUNK), ln_v_b.reshape(1, D_CHUNK), w_s, b_s.T, *extra_in)


MASKED_LOGIT = -1e30


def _sb_scores(q, kb, bias, mask):
    z = lax.dot_general(q, kb, (((1,), (1,)), ((), ())), preferred_element_type=F32) + bias
    sp = _softplus(z)
    ls = z - sp
    if mask is not None:
        sp = jnp.where(mask, sp, 0.0)
        ls = jnp.where(mask, ls, MASKED_LOGIT)
    return ls, sp.astype(BF16)


def _sb_weights(ls, spb, vb, upper, ones, carry):
    later = jnp.dot(spb, upper, preferred_element_type=F32)
    x = ls - later - jnp.concatenate([carry] * (ls.shape[1] // carry.shape[1]), axis=1)
    pv = jnp.dot(jnp.exp(x).astype(BF16), vb, preferred_element_type=F32)
    del ones
    tot = jnp.broadcast_to(later[:, :1] + spb[:, :1].astype(F32), carry.shape)
    return pv, tot


def _sb_prompt_kernel(bias_ref, q_ref, k_ref, v_ref, mix_ref, *rest, tq, hp, counts):
    del mix_ref
    nc = len(counts)
    srcs, o_ref, dsts = rest[:nc], rest[nc], rest[nc + 1:2 * nc + 1]
    kb_ref, vb_ref, acc_ref, carry_ref, ls_ref, sp_ref = rest[2 * nc + 1:]
    g = pl.program_id(1)
    i = pl.program_id(2)
    step = (pl.program_id(0) * pl.num_programs(1) + g) * pl.num_programs(2) + i
    _run_casts(step, srcs, dsts, counts)

    @pl.when(i == 0)
    def _():
        kb_ref[...] = k_ref[...].astype(BF16)
        vb_ref[...] = v_ref[...].astype(BF16)

    row = lax.broadcasted_iota(jnp.int32, (tq, tq), 0)
    col = lax.broadcasted_iota(jnp.int32, (tq, tq), 1)
    upper = (row > col).astype(BF16)
    ones = jnp.ones((tq, HEAD_DIM), BF16)
    qs = [(q_ref[:, h * HEAD_DIM:(h + 1) * HEAD_DIM] * ATT_SCALE).astype(BF16) for h in range(hp)]
    biases = [bias_ref[g * hp + h] for h in range(hp)]

    def key_rows(m):
        return pl.ds(pl.multiple_of((i - m) * tq, tq), tq)

    def stage1(m, mask):
        rows, slot = key_rows(m), m % 2
        for h in range(hp):
            ls, spb = _sb_scores(qs[h], kb_ref[rows, pl.ds(h * HEAD_DIM, HEAD_DIM)], biases[h], mask)
            ls_ref[slot, h] = ls
            sp_ref[slot, h] = spb

    def stage2(m):
        rows, slot = key_rows(m), m % 2
        for h in range(hp):
            pv, tot = _sb_weights(ls_ref[slot, h], sp_ref[slot, h],
                                  vb_ref[rows, pl.ds(h * HEAD_DIM, HEAD_DIM)], upper, ones, carry_ref[h])
            acc_ref[h] += pv
            carry_ref[h] += tot

    acc_ref[...] = jnp.zeros_like(acc_ref)
    carry_ref[...] = jnp.zeros_like(carry_ref)
    stage1(0, col < row)

    def body(m, c):
        stage2(m - 1)
        stage1(m, None)
        return c

    lax.fori_loop(1, i + 1, body, 0)
    stage2(i)
    for h in range(hp):
        o_ref[:, h * HEAD_DIM:(h + 1) * HEAD_DIM] = acc_ref[h].astype(o_ref.dtype)


SB_QUERY_TILE = 256
SB_HEADS_PER_STEP = 4


def _sb_prompt(p, sb_bias, mix, casts=(), tq=SB_QUERY_TILE, hp=SB_HEADS_PER_STEP):
    B, T, _ = p.shape
    w = hp * HEAD_DIM
    ng, ni = N_HEADS_ATT // hp, T // tq
    c_in, c_out, c_shapes, counts = _cast_jobs(casts, lambda b, g, i: (b * ng + g) * ni + i)
    return pl.pallas_call(
        functools.partial(_sb_prompt_kernel, tq=tq, hp=hp, counts=tuple(counts)),
        grid=(B, ng, ni),
        in_specs=[
            pl.BlockSpec(memory_space=pltpu.SMEM),
            pl.BlockSpec((None, tq, w), lambda b, g, i: (b, i, Q_COL128 // hp + g)),
            pl.BlockSpec((None, T, w), lambda b, g, i: (b, 0, K_COL128 // hp + g)),
            pl.BlockSpec((None, T, w), lambda b, g, i: (b, 0, V_COL128 // hp + g)),
            pl.BlockSpec(memory_space=pl.ANY),
        ] + c_in,
        out_specs=[pl.BlockSpec((None, tq, w), lambda b, g, i: (b, i, D_POOL // w + g))] + c_out,
        out_shape=[jax.ShapeDtypeStruct((B, T, D_MODEL), BF16)] + c_shapes,
        input_output_aliases={4: 0},
        scratch_shapes=[
            pltpu.VMEM((T, w), BF16), pltpu.VMEM((T, w), BF16),
            pltpu.VMEM((hp, tq, HEAD_DIM), F32), pltpu.VMEM((hp, tq, HEAD_DIM), F32),
            pltpu.VMEM((2, hp, tq, tq), F32), pltpu.VMEM((2, hp, tq, tq), BF16),
        ],
        compiler_params=_params("arbitrary", "arbitrary", "arbitrary"),
        name="sb_prompt",
    )(sb_bias, p, p, p, mix, *[job[0] for job in casts])


def _sb_sample_kernel(pt_ref, bias_ref, q_ref, *refs, n_steps, pps, page):
    del pt_ref
    k_refs, v_refs = refs[:pps], refs[pps:2 * pps]
    o_ref, carry_ref, acc_ref = refs[2 * pps:]
    j = pl.program_id(1)
    H = N_HEADS_ATT
    n = page * H
    lane = lax.broadcasted_iota(jnp.int32, (pps, n), 1)
    own = (lax.broadcasted_iota(jnp.int32, (H, n), 1) % H) == lax.broadcasted_iota(jnp.int32, (H, n), 0)

    @pl.when(j == 0)
    def _():
        carry_ref[...] = jnp.zeros_like(carry_ref)
        acc_ref[...] = jnp.zeros_like(acc_ref)

    q = (q_ref[...] * ATT_SCALE).astype(BF16)
    zs = []
    for pg in range(pps):
        kb = k_refs[pg][...].reshape(n, HEAD_DIM).astype(BF16)
        zt = lax.dot_general(q, kb, (((1,), (1,)), ((), ())), preferred_element_type=F32)
        zs.append(jnp.sum(jnp.where(own, zt, 0.0), axis=0, keepdims=True))
    z = jnp.concatenate(zs, axis=0) + bias_ref[...]
    sp = _softplus(z)
    later = jnp.where(lane < n - H, pltpu.roll(sp, n - H, 1), 0.0)
    total = sp
    step = H
    while step < n:
        later = later + jnp.where(lane < n - step, pltpu.roll(later, n - step, 1), 0.0)
        total = total + pltpu.roll(total, step, 1)
        step *= 2
    running = carry_ref[...]
    carries = [None] * pps
    for pg in reversed(range(pps)):
        carries[pg] = running
        running = running + total[pg:pg + 1]
    carry_ref[...] = running
    w = jnp.exp(z - sp - later - jnp.concatenate(carries, axis=0))
    acc = acc_ref[...]
    for pg in range(pps):
        vb = v_refs[pg][...].reshape(n, HEAD_DIM).astype(BF16)
        wm = jnp.where(own, jnp.broadcast_to(w[pg:pg + 1], (H, n)), 0.0).astype(BF16)
        acc = acc + jnp.dot(wm, vb, preferred_element_type=F32)
    acc_ref[...] = acc

    @pl.when(j == n_steps - 1)
    def _():
        o_ref[...] = acc


def _sb_sample(q, cache_k, cache_v, page_table, sb_bias, layer, pps=8):
    n_seq, n_pages = page_table.shape
    page = cache_k.shape[2]
    n_steps = n_pages // pps
    H = N_HEADS_ATT

    def cache_spec(pg):
        return pl.BlockSpec(
            (None, None, page, H, HEAD_DIM),
            lambda b, j, pt: (layer, pt[b, n_pages - (j + 1) * pps + pg], 0, 0, 0))

    grid_spec = pltpu.PrefetchScalarGridSpec(
        num_scalar_prefetch=1,
        grid=(n_seq, n_steps),
        in_specs=[
            pl.BlockSpec((1, page * H), lambda b, j, pt: (0, 0)),
            pl.BlockSpec((None, H, HEAD_DIM), lambda b, j, pt: (b, 0, 0)),
        ] + [cache_spec(pg) for pg in range(pps)] * 2,
        out_specs=pl.BlockSpec((None, H, HEAD_DIM), lambda b, j, pt: (b, 0, 0)),
        scratch_shapes=[pltpu.VMEM((1, page * H), F32), pltpu.VMEM((H, HEAD_DIM), F32)],
    )
    bias_row = jnp.tile(sb_bias, page).reshape(1, page * H)
    return pl.pallas_call(
        functools.partial(_sb_sample_kernel, n_steps=n_steps, pps=pps, page=page),
        grid_spec=grid_spec,
        out_shape=jax.ShapeDtypeStruct((n_seq, H, HEAD_DIM), F32),
        compiler_params=_params("parallel", "arbitrary"),
        name="sb_sample",
    )(page_table, bias_row, q, *([cache_k] * pps), *([cache_v] * pps))


def _mix_sample_kernel(p_ref, att_ref, state_ref, wp_ref, scale_ref, g_ref, b_ref, w0_ref, b0_ref,
                       cv_ref, o_ref):
    xa = p_ref[:, 0:D_POOL]
    outs = []
    for g, win in enumerate(POOL_WINDOWS):
        lo, hi = g * POOL_GROUP, (g + 1) * POOL_GROUP
        acc = xa[:, lo:hi]
        for back in range(1, win):
            acc = acc + state_ref[POOL_BUF - back, :, lo:hi]
        diff = acc / float(win) - xa[:, lo:hi]
        outs.append(jnp.dot(diff.astype(BF16), wp_ref[g].astype(BF16), preferred_element_type=F32))
    o_ref[:, 0:D_POOL] = (jnp.concatenate(outs, axis=-1) * scale_ref[...]).astype(o_ref.dtype)
    o_ref[:, D_POOL:D_POOL + D_ATT] = att_ref[...].astype(o_ref.dtype)
    u0 = D_POOL + 3 * D_ATT
    for h in range(N_HEADS_CHUNK):
        lo, hi = h * HEAD_DIM, (h + 1) * HEAD_DIM
        gvn = _layer_norm_rows(_gelu_tanh(p_ref[:, u0 + D_CHUNK + lo:u0 + D_CHUNK + hi]),
                               g_ref[:, lo:hi], b_ref[:, lo:hi])
        cv_ref[:, lo:hi] = gvn
        mixed = w0_ref[:, lo:hi] * gvn + b0_ref[:, lo:hi]
        o_ref[:, D_POOL + D_ATT + lo:D_POOL + D_ATT + hi] = (
            _gelu_tanh(p_ref[:, u0 + lo:u0 + hi]) * mixed).astype(o_ref.dtype)


def _mix_sample(p, att, state, w_pool, pool_scale, ln_v_g, ln_v_b, w_s, b_s):
    S = p.shape[0]
    w0 = jnp.repeat(w_s[:, 0, 0], HEAD_DIM).reshape(1, D_CHUNK)
    b0 = jnp.repeat(b_s[:, 0], HEAD_DIM).reshape(1, D_CHUNK)
    return pl.pallas_call(
        _mix_sample_kernel,
        out_shape=[
            jax.ShapeDtypeStruct((S, D_CHUNK), F32),
            jax.ShapeDtypeStruct((S, D_MODEL), F32),
        ],
        compiler_params=pltpu.CompilerParams(vmem_limit_bytes=VMEM_LIMIT_BYTES),
        name="mix_sample",
    )(p, att, jnp.swapaxes(state, 0, 1), w_pool, pool_scale.reshape(1, D_POOL), ln_v_g.reshape(1, D_CHUNK),
      ln_v_b.reshape(1, D_CHUNK), w0, b0)


def kernel(x_prompt, x_sample, cache_k, cache_v, state_pool, page_table, c_prompt, c_sample,
           ln0_g, ln0_b, w_ada, b_ada, w_in, sb_bias, w_pool, pool_scale, ln_v_g, ln_v_b, w_s, b_s,
           w_out, ln1_g, ln1_b, w_ff1, w_ff2, ln2_g, ln2_b):
    B, T, D = x_prompt.shape
    S = x_sample.shape[0]
    H = N_HEADS_ATT
    wb_in = w_in[0].astype(BF16)
    wb_out = None

    pad = (-(B + S)) % 8
    cs = jnp.concatenate([c_prompt, c_sample, jnp.zeros((pad, D), F32)], axis=0)
    mod = _ada(cs, w_ada, b_ada)
    mod_p = [mod[l, :B].reshape(B, 1, 6 * D) for l in range(DEPTH)]
    mod_s = [mod[l, B:B + S].reshape(1, S, 6 * D) for l in range(DEPTH)]
    xs = x_sample.reshape(1, S, D)

    xp, hp = _ln_mod(x_prompt, ln0_g, ln0_b, mod=mod_p[0], sc_col=1)
    xs, hs = _ln_mod(xs, ln0_g, ln0_b, mod=mod_s[0], sc_col=1, h_dtype=F32)

    k_st = v_st = cv_st = None
    outs = {k: [] for k in ("pp", "ks", "vs", "ps", "cs")}
    for l in range(DEPTH):
        if wb_out is None:
            p, ps, wb_out = _dense(hp.reshape(B * T, D), hs.reshape(S, D), wb_in,
                                   casts=[(w_out, l, 64, D)])
        else:
            p, ps = _dense(hp.reshape(B * T, D), hs.reshape(S, D), wb_in)
        p = p.reshape(B, T, D_IN)

        mix = _pool_prompt(p, w_pool[l], pool_scale[l])
        sb_steps = B * (H // SB_HEADS_PER_STEP) * (T // SB_QUERY_TILE)
        mix, wb_ff1 = _sb_prompt(p, sb_bias[l], mix, casts=[(w_ff1, l, D // sb_steps, D_FF)])
        cv_st, mix = _chunk_prompt(p, ln_v_g[l], ln_v_b[l], w_s[l], b_s[l], l, cv_st, mix)
        outs["pp"].append(p[:, T - POOL_BUF:, :D_POOL])
        mix = mix.reshape(B * T, D)

        q_s = ps[:, D_POOL:D_POOL + D_ATT].reshape(S, H, HEAD_DIM)
        att = _sb_sample(q_s, cache_k, cache_v, page_table, sb_bias[l], l).reshape(S, D_ATT)
        cv_s, mix_s = _mix_sample(ps, att, state_pool[l], w_pool[l], pool_scale[l], ln_v_g[l],
                                  ln_v_b[l], w_s[l], b_s[l])
        outs["ks"].append(ps[:, D_POOL + D_ATT:D_POOL + 2 * D_ATT].reshape(S, 1, H, HEAD_DIM))
        outs["vs"].append(ps[:, D_POOL + 2 * D_ATT:D_POOL + 3 * D_ATT].reshape(S, 1, H, HEAD_DIM))
        outs["ps"].append(jnp.concatenate([state_pool[l][:, 1:], ps[:, None, :D_POOL]], axis=1))
        outs["cs"].append(cv_s.reshape(S, 1, N_HEADS_CHUNK, HEAD_DIM))

        mo, mo_s = _dense(mix, mix_s, wb_out, out_dtype=BF16)
        xp, h2p = _ln_mod(xp, ln1_g[l], ln1_b[l], y=mo.reshape(B, T, D), mod_res=mod_p[l],
                          gate_col=2, mod=mod_p[l], sc_col=4)
        xs, h2s = _ln_mod(xs, ln1_g[l], ln1_b[l], y=mo_s.reshape(1, S, D), mod_res=mod_s[l],
                          gate_col=2, mod=mod_s[l], sc_col=4, h_dtype=F32)

        nxt = l + 1 < DEPTH
        p2 = p.reshape(B * T, D_IN)
        kv_rows = 2 * B * T // ((D_FF // 1024) * (B * T // 1024))
        a, a_s, wb_ff2, k_st, v_st = _dense(
            h2p.reshape(B * T, D), h2s.reshape(S, D), wb_ff1, out_dtype=BF16, relu2=True,
            casts=[(w_ff2, l, 128, D), (p2, K_COL1024, l, kv_rows, k_st), (p2, V_COL1024, l, kv_rows, v_st)])
        if nxt:
            f, wb_in, wb_out = _matmul(a, wb_ff2, tm=1024, tn=1024, tk=D,
                                       casts=[(w_in, l + 1, 32, D_IN), (w_out, l + 1, 32, D)])
        else:
            f = _matmul(a, wb_ff2, tm=1024, tn=1024, tk=D)
        f = f.reshape(B, T, D)
        f_s = _matmul(a_s, wb_ff2, tm=S, tn=1024, tk=D).reshape(1, S, D)
        xp, hp = _ln_mod(xp, ln2_g[l], ln2_b[l], y=f, mod_res=mod_p[l], gate_col=5,
                         mod=mod_p[l + 1] if nxt else None, sc_col=1)
        xs, hs = _ln_mod(xs, ln2_g[l], ln2_b[l], y=f_s, mod_res=mod_s[l], gate_col=5,
                         mod=mod_s[l + 1] if nxt else None, sc_col=1, h_dtype=F32)

    st = {k: jnp.stack(v) for k, v in outs.items()}
    k_st = k_st.reshape(DEPTH, B, T, H, HEAD_DIM)
    v_st = v_st.reshape(DEPTH, B, T, H, HEAD_DIM)
    return (xp, xs.reshape(S, 1, D), k_st, v_st, st["pp"], cv_st,
            st["ks"], st["vs"], st["ps"], st["cs"])
```

```python
import functools
import math

import jax
import jax.numpy as jnp
from jax import lax
from jax.experimental import pallas as pl
from jax.experimental.pallas import tpu as pltpu

F32 = jnp.float32
BF16 = jnp.bfloat16

D_MODEL = 4096
HEAD_DIM = 128
D_ATT = D_MODEL // 2
D_POOL = D_MODEL // 4
D_CHUNK = D_MODEL // 4
N_HEADS_ATT = D_ATT // HEAD_DIM
N_HEADS_CHUNK = D_CHUNK // HEAD_DIM
POOL_WINDOWS = (2, 4, 8, 16)
POOL_GROUP = D_POOL // len(POOL_WINDOWS)
POOL_BUF = max(POOL_WINDOWS) - 1
POOL_HALO = 16
CHUNK = 128
D_FF = 4 * D_MODEL
D_IN = D_POOL + 3 * D_ATT + 2 * D_CHUNK
DEPTH = 2
ALPHA = (2.0 * DEPTH) ** 0.25
LN_EPS = 1e-5
ATT_SCALE = 1.0 / math.sqrt(HEAD_DIM)
LOG2E = 1.0 / math.log(2.0)

Q_COL128 = D_POOL // HEAD_DIM
K_COL128 = (D_POOL + D_ATT) // HEAD_DIM
V_COL128 = (D_POOL + 2 * D_ATT) // HEAD_DIM
K_COL1024 = (D_POOL + D_ATT) // 1024
V_COL1024 = (D_POOL + 2 * D_ATT) // 1024
U_COL1024 = (D_POOL + 3 * D_ATT) // D_CHUNK
GV_COL1024 = U_COL1024 + 1

VMEM_LIMIT_BYTES = 56 * 1024 * 1024


def _params(*sem):
    return pltpu.CompilerParams(dimension_semantics=sem, vmem_limit_bytes=VMEM_LIMIT_BYTES)


def _gelu_tanh(x):
    c = math.sqrt(2.0 / math.pi)
    return 0.5 * x * (1.0 + jnp.tanh(c * (x + 0.044715 * (x * x * x))))


def _softplus(z):
    return jnp.maximum(z, 0.0) + jnp.log(1.0 + jnp.exp2(jnp.abs(z) * (-LOG2E)))


def _layer_norm_rows(x, g, b):
    mu = jnp.mean(x, axis=-1, keepdims=True)
    xc = x - mu
    var = jnp.mean(xc * xc, axis=-1, keepdims=True)
    return xc * lax.rsqrt(var + LN_EPS) * g + b


def _ada_kernel(c_ref, w_ref, b_ref, o_ref):
    c = c_ref[...]
    s = (c * jax.nn.sigmoid(c)).astype(BF16)
    o_ref[...] = jnp.dot(s, w_ref[...].astype(BF16), preferred_element_type=F32) + b_ref[...]


def _ada(cs, w_ada, b_ada, tn=512):
    rows = cs.shape[0]
    n = w_ada.shape[-1]
    return pl.pallas_call(
        _ada_kernel,
        grid=(DEPTH, n // tn),
        in_specs=[
            pl.BlockSpec((rows, D_MODEL), lambda l, j: (0, 0)),
            pl.BlockSpec((None, D_MODEL, tn), lambda l, j: (l, 0, j)),
            pl.BlockSpec((None, 1, tn), lambda l, j: (l, 0, j)),
        ],
        out_specs=pl.BlockSpec((None, rows, tn), lambda l, j: (l, 0, j)),
        out_shape=jax.ShapeDtypeStruct((DEPTH, rows, n), F32),
        compiler_params=_params("arbitrary", "arbitrary"),
        name="ada_mod",
    )(cs, w_ada, b_ada.reshape(DEPTH, 1, n))


def _ln_mod_kernel(*refs, has_res, has_mod):
    it = iter(refs)
    x_ref = next(it)
    if has_res:
        y_ref, gate_ref = next(it), next(it)
    g_ref, b_ref = next(it), next(it)
    if has_mod:
        sc_ref, sh_ref = next(it), next(it)
    xo_ref = next(it)
    x = x_ref[...]
    if has_res:
        x = ALPHA * x + gate_ref[...] * y_ref[...].astype(F32)
    xn = _layer_norm_rows(x, g_ref[...], b_ref[...])
    xo_ref[...] = xn
    if has_mod:
        ho_ref = next(it)
        ho_ref[...] = (xn * (1.0 + sc_ref[...]) + sh_ref[...]).astype(ho_ref.dtype)


def _ln_mod(x, ln_g, ln_b, *, y=None, mod=None, gate_col=None, mod_res=None, sc_col=None,
            h_dtype=BF16, tt=256):
    B, T, D = x.shape
    tt = min(tt, T)
    has_res = y is not None
    has_mod = mod is not None
    row = pl.BlockSpec((None, tt, D), lambda b, i: (b, i, 0))
    vec = pl.BlockSpec((1, D), lambda b, i: (0, 0))

    def modspec(m, col):
        tm = m.shape[1]
        return pl.BlockSpec((None, tm, D), lambda b, i: (b, 0, col))

    args, specs = [x], [row]
    if has_res:
        args += [y, mod_res]
        specs += [row, modspec(mod_res, gate_col)]
    args += [ln_g.reshape(1, D), ln_b.reshape(1, D)]
    specs += [vec, vec]
    if has_mod:
        args += [mod, mod]
        specs += [modspec(mod, sc_col), modspec(mod, sc_col - 1)]
    out_shape = [jax.ShapeDtypeStruct((B, T, D), F32)]
    out_specs = [row]
    if has_mod:
        out_shape.append(jax.ShapeDtypeStruct((B, T, D), h_dtype))
        out_specs.append(row)
    res = pl.pallas_call(
        functools.partial(_ln_mod_kernel, has_res=has_res, has_mod=has_mod),
        grid=(B, T // tt),
        in_specs=specs,
        out_specs=out_specs,
        out_shape=out_shape,
        compiler_params=_params("parallel", "parallel"),
        name="ln_mod",
    )(*args)
    return (res[0], res[1]) if has_mod else (res[0], None)


def _relu2(acc):
    r = jnp.maximum(acc, 0.0)
    return r * r


def _cast_jobs(jobs, step_of):
    in_specs, out_specs, out_shapes, counts = [], [], [], []
    for job in jobs:
        if len(job) == 5:
            src, col, layer, rb, _ = job
            rows = src.shape[0]
            count = 2 * (rows // rb)

            def half(*g, count=count):
                s = jnp.minimum(step_of(*g), count - 1)
                return s // 2, s % 2

            in_specs.append(pl.BlockSpec(
                (rb, 1024), lambda *g, half=half, col=col: (half(*g)[0], col + half(*g)[1])))
            out_specs.append(pl.BlockSpec(
                (None, rb, N_HEADS_ATT // 2, HEAD_DIM),
                lambda *g, half=half, layer=layer: (layer, *half(*g), 0)))
            out_shapes.append(jax.ShapeDtypeStruct((DEPTH, rows, N_HEADS_ATT, HEAD_DIM), F32))
            counts.append(count)
            continue
        src, layer, rb, cb = job
        _, rows, cols = src.shape
        ncb = cols // cb
        count = (rows // rb) * ncb

        def block(*g, count=count, ncb=ncb):
            s = jnp.minimum(step_of(*g), count - 1)
            return s // ncb, s % ncb

        in_specs.append(pl.BlockSpec((None, rb, cb),
                                     lambda *g, block=block, layer=layer: (layer, *block(*g))))
        out_specs.append(pl.BlockSpec((rb, cb), lambda *g, block=block: block(*g)))
        out_shapes.append(jax.ShapeDtypeStruct((rows, cols), BF16))
        counts.append(count)
    return in_specs, out_specs, out_shapes, counts


def _run_casts(step, srcs, dsts, counts):
    for src, dst, count in zip(srcs, dsts, counts):
        @pl.when(step < count)
        def _(src=src, dst=dst):
            dst[...] = src[...].astype(dst.dtype).reshape(dst.shape)


def _dense_kernel(x_ref, xs_ref, w_ref, *rest, relu2, counts, ni, n_prev):
    nc = len(counts)
    srcs, rest = rest[:nc], rest[nc + n_prev:]
    (o_ref, os_ref), dsts = rest[:2], rest[2:]
    post = _relu2 if relu2 else (lambda a: a)
    j, i = pl.program_id(0), pl.program_id(1)

    @pl.when(i == 0)
    def _():
        os_ref[...] = post(jnp.dot(xs_ref[...].astype(BF16), w_ref[...],
                                   preferred_element_type=F32)).astype(os_ref.dtype)

    o_ref[...] = post(jnp.dot(x_ref[...], w_ref[...],
                              preferred_element_type=F32)).astype(o_ref.dtype)
    _run_casts(j * ni + i, srcs, dsts, counts)


def _dense(x, xs, w, *, out_dtype=F32, relu2=False, casts=(), tm=1024, tn=1024):
    M, K = x.shape
    S = xs.shape[0]
    N = w.shape[1]
    ni = M // tm
    c_in, c_out, c_shapes, counts = _cast_jobs(casts, lambda j, i: j * ni + i)
    prevs = [(n, job[4]) for n, job in enumerate(casts) if len(job) == 5 and job[4] is not None]
    n_fixed_in, n_fixed_out = 3, 2
    aliases = {n_fixed_in + len(casts) + k: n_fixed_out + n for k, (n, _) in enumerate(prevs)}
    return pl.pallas_call(
        functools.partial(_dense_kernel, relu2=relu2, counts=tuple(counts), ni=ni, n_prev=len(prevs)),
        grid=(N // tn, ni),
        in_specs=[
            pl.BlockSpec((tm, K), lambda j, i: (i, 0)),
            pl.BlockSpec((S, K), lambda j, i: (0, 0)),
            pl.BlockSpec((K, tn), lambda j, i: (0, j)),
        ] + c_in + [pl.BlockSpec(memory_space=pl.ANY)] * len(prevs),
        out_specs=[
            pl.BlockSpec((tm, tn), lambda j, i: (i, j)),
            pl.BlockSpec((S, tn), lambda j, i: (0, j)),
        ] + c_out,
        out_shape=[jax.ShapeDtypeStruct((M, N), out_dtype), jax.ShapeDtypeStruct((S, N), F32)] + c_shapes,
        input_output_aliases=aliases,
        compiler_params=_params("arbitrary", "arbitrary"),
        name="dense",
    )(x, xs, w, *[job[0] for job in casts], *[prev for _, prev in prevs])


def _mm_kernel(x_ref, w_ref, *rest, nk, nj, relu2, counts):
    nc = len(counts)
    srcs, o_ref, dsts = rest[:nc], rest[nc], rest[nc + 1:2 * nc + 1]
    i, j, k = pl.program_id(0), pl.program_id(1), pl.program_id(2)

    def finish(acc):
        o_ref[...] = (_relu2(acc) if relu2 else acc).astype(o_ref.dtype)

    if nk == 1:
        finish(jnp.dot(x_ref[...].astype(BF16), w_ref[...], preferred_element_type=F32))
    else:
        acc_ref = o_ref if (o_ref.dtype == F32 and not relu2) else rest[-1]

        @pl.when(k == 0)
        def _():
            acc_ref[...] = jnp.zeros_like(acc_ref)

        acc_ref[...] += jnp.dot(x_ref[...].astype(BF16), w_ref[...], preferred_element_type=F32)

        if acc_ref is not o_ref:
            @pl.when(k == nk - 1)
            def _():
                finish(acc_ref[...])

    _run_casts((i * nj + j) * nk + k, srcs, dsts, counts)


def _matmul(x, w, *, tm, tn, tk, out_dtype=F32, relu2=False, casts=()):
    M, K = x.shape
    N = w.shape[1]
    tm, tn, tk = min(tm, M), min(tn, N), min(tk, K)
    nj, nk = N // tn, K // tk
    needs_acc = nk > 1 and (out_dtype != F32 or relu2)
    scratch = [pltpu.VMEM((tm, tn), F32)] if needs_acc else []
    c_in, c_out, c_shapes, counts = _cast_jobs(casts, lambda i, j, k: (i * nj + j) * nk + k)
    res = pl.pallas_call(
        functools.partial(_mm_kernel, nk=nk, nj=nj, relu2=relu2, counts=tuple(counts)),
        grid=(M // tm, nj, nk),
        in_specs=[
            pl.BlockSpec((tm, tk), lambda i, j, k: (i, k)),
            pl.BlockSpec((tk, tn), lambda i, j, k: (k, j)),
        ] + c_in,
        out_specs=[pl.BlockSpec((tm, tn), lambda i, j, k: (i, j))] + c_out,
        out_shape=[jax.ShapeDtypeStruct((M, N), out_dtype)] + c_shapes,
        scratch_shapes=scratch,
        compiler_params=_params("arbitrary", "arbitrary", "arbitrary"),
        name="dense_matmul",
    )(x, w, *[job[0] for job in casts])
    return res if casts else res[0]


def _pool_prompt_kernel(xa_ref, halo_ref, w_ref, scale_ref, o_ref, ext_ref, *, tt):
    i = pl.program_id(1)
    xa = xa_ref[...]
    ext_ref[pl.ds(POOL_HALO, tt), :] = xa

    @pl.when(i == 0)
    def _():
        ext_ref[pl.ds(0, POOL_HALO), :] = jnp.zeros((POOL_HALO, D_POOL), F32)

    @pl.when(i > 0)
    def _():
        ext_ref[pl.ds(0, POOL_HALO), :] = halo_ref[...]

    pos = i * tt + lax.broadcasted_iota(jnp.int32, (tt, 1), 0)
    outs = []
    for g, win in enumerate(POOL_WINDOWS):
        cols = pl.ds(g * POOL_GROUP, POOL_GROUP)
        acc = ext_ref[pl.ds(POOL_HALO, tt), cols]
        for back in range(1, win):
            acc = acc + ext_ref[pl.ds(POOL_HALO - back, tt), cols]
        cnt = jnp.minimum(win, pos + 1).astype(F32)
        diff = acc / cnt - xa[:, g * POOL_GROUP:(g + 1) * POOL_GROUP]
        outs.append(jnp.dot(diff.astype(BF16), w_ref[g].astype(BF16), preferred_element_type=F32))
    y = jnp.concatenate(outs, axis=-1) * scale_ref[...]
    o_ref[...] = y.astype(o_ref.dtype)


def _pool_prompt(p, w_pool, pool_scale, tt=256):
    B, T, _ = p.shape
    halo_blocks = tt // POOL_HALO
    return pl.pallas_call(
        functools.partial(_pool_prompt_kernel, tt=tt),
        grid=(B, T // tt),
        in_specs=[
            pl.BlockSpec((None, tt, D_POOL), lambda b, i: (b, i, 0)),
            pl.BlockSpec((None, POOL_HALO, D_POOL),
                         lambda b, i: (b, jnp.maximum(i * halo_blocks - 1, 0), 0)),
            pl.BlockSpec((len(POOL_WINDOWS), POOL_GROUP, POOL_GROUP), lambda b, i: (0, 0, 0)),
            pl.BlockSpec((1, D_POOL), lambda b, i: (0, 0)),
        ],
        out_specs=pl.BlockSpec((None, tt, D_POOL), lambda b, i: (b, i, 0)),
        out_shape=jax.ShapeDtypeStruct((B, T, D_MODEL), BF16),
        scratch_shapes=[pltpu.VMEM((POOL_HALO + tt, D_POOL), F32)],
        compiler_params=_params("parallel", "arbitrary"),
        name="pool_prompt",
    )(p, p, w_pool, pool_scale.reshape(1, D_POOL))


def _stacked(layer, prev, shape):
    extra_in = [] if prev is None else [prev]
    extra_specs = [] if prev is None else [pl.BlockSpec(memory_space=pl.ANY)]
    return extra_in, extra_specs, jax.ShapeDtypeStruct(shape, F32)


def _chunk_prompt_kernel(u_ref, gv_ref, g_ref, b_ref, ws_ref, bs_ref, *rest):
    cv_ref, o_ref = rest[-2:]
    row = lax.broadcasted_iota(jnp.int32, (CHUNK, CHUNK), 0)
    col = lax.broadcasted_iota(jnp.int32, (CHUNK, CHUNK), 1)
    tril = col <= row
    for h in range(N_HEADS_CHUNK):
        cols = pl.ds(h * HEAD_DIM, HEAD_DIM)
        gvn = _layer_norm_rows(_gelu_tanh(gv_ref[:, cols]), g_ref[:, cols], b_ref[:, cols])
        cv_ref[:, h, :] = gvn
        w = jnp.where(tril, ws_ref[h], 0.0).astype(BF16)
        mixed = jnp.dot(w, gvn.astype(BF16), preferred_element_type=F32) + bs_ref[:, h:h + 1]
        o_ref[:, cols] = (_gelu_tanh(u_ref[:, cols]) * mixed).astype(o_ref.dtype)


def _chunk_prompt(p, ln_v_g, ln_v_b, w_s, b_s, layer, cv_prev, mix):
    B, T, _ = p.shape
    vec = pl.BlockSpec((1, D_CHUNK), lambda b, i: (0, 0))
    extra_in, extra_specs, cv_shape = _stacked(
        layer, cv_prev, (DEPTH, B, T, N_HEADS_CHUNK, HEAD_DIM))
    n_in = 6
    aliases = {n_in: 1} if cv_prev is None else {n_in: 0, n_in + 1: 1}
    extra_in = extra_in + [mix]
    extra_specs = extra_specs + [pl.BlockSpec(memory_space=pl.ANY)]
    return pl.pallas_call(
        _chunk_prompt_kernel,
        grid=(B, T // CHUNK),
        in_specs=[
            pl.BlockSpec((None, CHUNK, D_CHUNK), lambda b, i: (b, i, U_COL1024)),
            pl.BlockSpec((None, CHUNK, D_CHUNK), lambda b, i: (b, i, GV_COL1024)),
            vec, vec,
            pl.BlockSpec((N_HEADS_CHUNK, CHUNK, CHUNK), lambda b, i: (0, 0, 0)),
            pl.BlockSpec((CHUNK, N_HEADS_CHUNK), lambda b, i: (0, 0)),
        ] + extra_specs,
        out_specs=[
            pl.BlockSpec((None, None, CHUNK, N_HEADS_CHUNK, HEAD_DIM),
                         lambda b, i: (layer, b, i, 0, 0)),
            pl.BlockSpec((None, CHUNK, D_CHUNK), lambda b, i: (b, i, (D_POOL + D_ATT) // D_CHUNK)),
        ],
        out_shape=[cv_shape, jax.ShapeDtypeStruct((B, T, D_MODEL), BF16)],
        input_output_aliases=aliases,
        compiler_params=_params("parallel", "parallel"),
        name="chunk_prompt",
    )(p, p, ln_v_g.reshape(1, D_CHUNK), ln_v_b.reshape(1, D_CHUNK), w_s, b_s.T, *extra_in)


MASKED_LOGIT = -1e30


def _sb_scores(q, kb, bias, mask):
    z = lax.dot_general(q, kb, (((1,), (1,)), ((), ())), preferred_element_type=F32) + bias
    sp = _softplus(z)
    ls = z - sp
    if mask is not None:
        sp = jnp.where(mask, sp, 0.0)
        ls = jnp.where(mask, ls, MASKED_LOGIT)
    return ls, sp.astype(BF16)


def _sb_weights(ls, spb, vb, upper, ones, carry):
    later = jnp.dot(spb, upper, preferred_element_type=F32)
    x = ls - later - jnp.concatenate([carry] * (ls.shape[1] // carry.shape[1]), axis=1)
    pv = jnp.dot(jnp.exp(x).astype(BF16), vb, preferred_element_type=F32)
    del ones
    tot = jnp.broadcast_to(later[:, :1] + spb[:, :1].astype(F32), carry.shape)
    return pv, tot


def _sb_prompt_kernel(bias_ref, q_ref, k_ref, v_ref, mix_ref, *rest, tq, hp, counts):
    del mix_ref
    nc = len(counts)
    srcs, o_ref, dsts = rest[:nc], rest[nc], rest[nc + 1:2 * nc + 1]
    kb_ref, vb_ref, acc_ref, carry_ref, ls_ref, sp_ref = rest[2 * nc + 1:]
    g = pl.program_id(1)
    i = pl.program_id(2)
    step = (pl.program_id(0) * pl.num_programs(1) + g) * pl.num_programs(2) + i
    _run_casts(step, srcs, dsts, counts)

    @pl.when(i == 0)
    def _():
        kb_ref[...] = k_ref[...].astype(BF16)
        vb_ref[...] = v_ref[...].astype(BF16)

    row = lax.broadcasted_iota(jnp.int32, (tq, tq), 0)
    col = lax.broadcasted_iota(jnp.int32, (tq, tq), 1)
    upper = (row > col).astype(BF16)
    ones = jnp.ones((tq, HEAD_DIM), BF16)
    qs = [(q_ref[:, h * HEAD_DIM:(h + 1) * HEAD_DIM] * ATT_SCALE).astype(BF16) for h in range(hp)]
    biases = [bias_ref[g * hp + h] for h in range(hp)]

    def key_rows(m):
        return pl.ds(pl.multiple_of((i - m) * tq, tq), tq)

    def stage1(m, mask):
        rows, slot = key_rows(m), m % 2
        for h in range(hp):
            ls, spb = _sb_scores(qs[h], kb_ref[rows, pl.ds(h * HEAD_DIM, HEAD_DIM)], biases[h], mask)
            ls_ref[slot, h] = ls
            sp_ref[slot, h] = spb

    def stage2(m):
        rows, slot = key_rows(m), m % 2
        for h in range(hp):
            pv, tot = _sb_weights(ls_ref[slot, h], sp_ref[slot, h],
                                  vb_ref[rows, pl.ds(h * HEAD_DIM, HEAD_DIM)], upper, ones, carry_ref[h])
            acc_ref[h] += pv
            carry_ref[h] += tot

    acc_ref[...] = jnp.zeros_like(acc_ref)
    carry_ref[...] = jnp.zeros_like(carry_ref)
    stage1(0, col < row)

    def body(m, c):
        stage2(m - 1)
        stage1(m, None)
        return c

    lax.fori_loop(1, i + 1, body, 0)
    stage2(i)
    for h in range(hp):
        o_ref[:, h * HEAD_DIM:(h + 1) * HEAD_DIM] = acc_ref[h].astype(o_ref.dtype)


SB_QUERY_TILE = 512
SB_HEADS_PER_STEP = 4


def _sb_prompt(p, sb_bias, mix, casts=(), tq=SB_QUERY_TILE, hp=SB_HEADS_PER_STEP):
    B, T, _ = p.shape
    w = hp * HEAD_DIM
    ng, ni = N_HEADS_ATT // hp, T // tq
    c_in, c_out, c_shapes, counts = _cast_jobs(casts, lambda b, g, i: (b * ng + g) * ni + i)
    return pl.pallas_call(
        functools.partial(_sb_prompt_kernel, tq=tq, hp=hp, counts=tuple(counts)),
        grid=(B, ng, ni),
        in_specs=[
            pl.BlockSpec(memory_space=pltpu.SMEM),
            pl.BlockSpec((None, tq, w), lambda b, g, i: (b, i, Q_COL128 // hp + g)),
            pl.BlockSpec((None, T, w), lambda b, g, i: (b, 0, K_COL128 // hp + g)),
            pl.BlockSpec((None, T, w), lambda b, g, i: (b, 0, V_COL128 // hp + g)),
            pl.BlockSpec(memory_space=pl.ANY),
        ] + c_in,
        out_specs=[pl.BlockSpec((None, tq, w), lambda b, g, i: (b, i, D_POOL // w + g))] + c_out,
        out_shape=[jax.ShapeDtypeStruct((B, T, D_MODEL), BF16)] + c_shapes,
        input_output_aliases={4: 0},
        scratch_shapes=[
            pltpu.VMEM((T, w), BF16), pltpu.VMEM((T, w), BF16),
            pltpu.VMEM((hp, tq, HEAD_DIM), F32), pltpu.VMEM((hp, tq, HEAD_DIM), F32),
            pltpu.VMEM((2, hp, tq, tq), F32), pltpu.VMEM((2, hp, tq, tq), BF16),
        ],
        compiler_params=_params("arbitrary", "arbitrary", "arbitrary"),
        name="sb_prompt",
    )(sb_bias, p, p, p, mix, *[job[0] for job in casts])


def _sb_sample_kernel(pt_ref, bias_ref, q_ref, *refs, n_steps, pps, page):
    del pt_ref
    k_refs, v_refs = refs[:pps], refs[pps:2 * pps]
    o_ref, carry_ref, acc_ref = refs[2 * pps:]
    j = pl.program_id(1)
    H = N_HEADS_ATT
    n = page * H
    lane = lax.broadcasted_iota(jnp.int32, (pps, n), 1)
    own = (lax.broadcasted_iota(jnp.int32, (H, n), 1) % H) == lax.broadcasted_iota(jnp.int32, (H, n), 0)

    @pl.when(j == 0)
    def _():
        carry_ref[...] = jnp.zeros_like(carry_ref)
        acc_ref[...] = jnp.zeros_like(acc_ref)

    q = (q_ref[...] * ATT_SCALE).astype(BF16)
    zs = []
    for pg in range(pps):
        kb = k_refs[pg][...].reshape(n, HEAD_DIM).astype(BF16)
        zt = lax.dot_general(q, kb, (((1,), (1,)), ((), ())), preferred_element_type=F32)
        zs.append(jnp.sum(jnp.where(own, zt, 0.0), axis=0, keepdims=True))
    z = jnp.concatenate(zs, axis=0) + bias_ref[...]
    sp = _softplus(z)
    later = jnp.where(lane < n - H, pltpu.roll(sp, n - H, 1), 0.0)
    total = sp
    step = H
    while step < n:
        later = later + jnp.where(lane < n - step, pltpu.roll(later, n - step, 1), 0.0)
        total = total + pltpu.roll(total, step, 1)
        step *= 2
    running = carry_ref[...]
    carries = [None] * pps
    for pg in reversed(range(pps)):
        carries[pg] = running
        running = running + total[pg:pg + 1]
    carry_ref[...] = running
    w = jnp.exp(z - sp - later - jnp.concatenate(carries, axis=0))
    acc = acc_ref[...]
    for pg in range(pps):
        vb = v_refs[pg][...].reshape(n, HEAD_DIM).astype(BF16)
        wm = jnp.where(own, jnp.broadcast_to(w[pg:pg + 1], (H, n)), 0.0).astype(BF16)
        acc = acc + jnp.dot(wm, vb, preferred_element_type=F32)
    acc_ref[...] = acc

    @pl.when(j == n_steps - 1)
    def _():
        o_ref[...] = acc


def _sb_sample(q, cache_k, cache_v, page_table, sb_bias, layer, pps=8):
    n_seq, n_pages = page_table.shape
    page = cache_k.shape[2]
    n_steps = n_pages // pps
    H = N_HEADS_ATT

    def cache_spec(pg):
        return pl.BlockSpec(
            (None, None, page, H, HEAD_DIM),
            lambda b, j, pt: (layer, pt[b, n_pages - (j + 1) * pps + pg], 0, 0, 0))

    grid_spec = pltpu.PrefetchScalarGridSpec(
        num_scalar_prefetch=1,
        grid=(n_seq, n_steps),
        in_specs=[
            pl.BlockSpec((1, page * H), lambda b, j, pt: (0, 0)),
            pl.BlockSpec((None, H, HEAD_DIM), lambda b, j, pt: (b, 0, 0)),
        ] + [cache_spec(pg) for pg in range(pps)] * 2,
        out_specs=pl.BlockSpec((None, H, HEAD_DIM), lambda b, j, pt: (b, 0, 0)),
        scratch_shapes=[pltpu.VMEM((1, page * H), F32), pltpu.VMEM((H, HEAD_DIM), F32)],
    )
    bias_row = jnp.tile(sb_bias, page).reshape(1, page * H)
    return pl.pallas_call(
        functools.partial(_sb_sample_kernel, n_steps=n_steps, pps=pps, page=page),
        grid_spec=grid_spec,
        out_shape=jax.ShapeDtypeStruct((n_seq, H, HEAD_DIM), F32),
        compiler_params=_params("parallel", "arbitrary"),
        name="sb_sample",
    )(page_table, bias_row, q, *([cache_k] * pps), *([cache_v] * pps))


def _mix_sample_kernel(p_ref, att_ref, state_ref, wp_ref, scale_ref, g_ref, b_ref, w0_ref, b0_ref,
                       cv_ref, o_ref):
    xa = p_ref[:, 0:D_POOL]
    outs = []
    for g, win in enumerate(POOL_WINDOWS):
        lo, hi = g * POOL_GROUP, (g + 1) * POOL_GROUP
        acc = xa[:, lo:hi]
        for back in range(1, win):
            acc = acc + state_ref[POOL_BUF - back, :, lo:hi]
        diff = acc / float(win) - xa[:, lo:hi]
        outs.append(jnp.dot(diff.astype(BF16), wp_ref[g].astype(BF16), preferred_element_type=F32))
    o_ref[:, 0:D_POOL] = (jnp.concatenate(outs, axis=-1) * scale_ref[...]).astype(o_ref.dtype)
    o_ref[:, D_POOL:D_POOL + D_ATT] = att_ref[...].astype(o_ref.dtype)
    u0 = D_POOL + 3 * D_ATT
    for h in range(N_HEADS_CHUNK):
        lo, hi = h * HEAD_DIM, (h + 1) * HEAD_DIM
        gvn = _layer_norm_rows(_gelu_tanh(p_ref[:, u0 + D_CHUNK + lo:u0 + D_CHUNK + hi]),
                               g_ref[:, lo:hi], b_ref[:, lo:hi])
        cv_ref[:, lo:hi] = gvn
        mixed = w0_ref[:, lo:hi] * gvn + b0_ref[:, lo:hi]
        o_ref[:, D_POOL + D_ATT + lo:D_POOL + D_ATT + hi] = (
            _gelu_tanh(p_ref[:, u0 + lo:u0 + hi]) * mixed).astype(o_ref.dtype)


def _mix_sample(p, att, state, w_pool, pool_scale, ln_v_g, ln_v_b, w_s, b_s):
    S = p.shape[0]
    w0 = jnp.repeat(w_s[:, 0, 0], HEAD_DIM).reshape(1, D_CHUNK)
    b0 = jnp.repeat(b_s[:, 0], HEAD_DIM).reshape(1, D_CHUNK)
    return pl.pallas_call(
        _mix_sample_kernel,
        out_shape=[
            jax.ShapeDtypeStruct((S, D_CHUNK), F32),
            jax.ShapeDtypeStruct((S, D_MODEL), F32),
        ],
        compiler_params=pltpu.CompilerParams(vmem_limit_bytes=VMEM_LIMIT_BYTES),
        name="mix_sample",
    )(p, att, jnp.swapaxes(state, 0, 1), w_pool, pool_scale.reshape(1, D_POOL), ln_v_g.reshape(1, D_CHUNK),
      ln_v_b.reshape(1, D_CHUNK), w0, b0)


def kernel(x_prompt, x_sample, cache_k, cache_v, state_pool, page_table, c_prompt, c_sample,
           ln0_g, ln0_b, w_ada, b_ada, w_in, sb_bias, w_pool, pool_scale, ln_v_g, ln_v_b, w_s, b_s,
           w_out, ln1_g, ln1_b, w_ff1, w_ff2, ln2_g, ln2_b):
    B, T, D = x_prompt.shape
    S = x_sample.shape[0]
    H = N_HEADS_ATT
    wb_in = w_in[0].astype(BF16)
    wb_out = None

    pad = (-(B + S)) % 8
    cs = jnp.concatenate([c_prompt, c_sample, jnp.zeros((pad, D), F32)], axis=0)
    mod = _ada(cs, w_ada, b_ada)
    mod_p = [mod[l, :B].reshape(B, 1, 6 * D) for l in range(DEPTH)]
    mod_s = [mod[l, B:B + S].reshape(1, S, 6 * D) for l in range(DEPTH)]
    xs = x_sample.reshape(1, S, D)

    xp, hp = _ln_mod(x_prompt, ln0_g, ln0_b, mod=mod_p[0], sc_col=1)
    xs, hs = _ln_mod(xs, ln0_g, ln0_b, mod=mod_s[0], sc_col=1, h_dtype=F32)

    k_st = v_st = cv_st = None
    outs = {k: [] for k in ("pp", "ks", "vs", "ps", "cs")}
    for l in range(DEPTH):
        if wb_out is None:
            p, ps, wb_out = _dense(hp.reshape(B * T, D), hs.reshape(S, D), wb_in,
                                   casts=[(w_out, l, 64, D)])
        else:
            p, ps = _dense(hp.reshape(B * T, D), hs.reshape(S, D), wb_in)
        p = p.reshape(B, T, D_IN)

        mix = _pool_prompt(p, w_pool[l], pool_scale[l])
        sb_steps = B * (H // SB_HEADS_PER_STEP) * (T // SB_QUERY_TILE)
        mix, wb_ff1 = _sb_prompt(p, sb_bias[l], mix, casts=[(w_ff1, l, D // sb_steps, D_FF)])
        cv_st, mix = _chunk_prompt(p, ln_v_g[l], ln_v_b[l], w_s[l], b_s[l], l, cv_st, mix)
        outs["pp"].append(p[:, T - POOL_BUF:, :D_POOL])
        mix = mix.reshape(B * T, D)

        q_s = ps[:, D_POOL:D_POOL + D_ATT].reshape(S, H, HEAD_DIM)
        att = _sb_sample(q_s, cache_k, cache_v, page_table, sb_bias[l], l).reshape(S, D_ATT)
        cv_s, mix_s = _mix_sample(ps, att, state_pool[l], w_pool[l], pool_scale[l], ln_v_g[l],
                                  ln_v_b[l], w_s[l], b_s[l])
        outs["ks"].append(ps[:, D_POOL + D_ATT:D_POOL + 2 * D_ATT].reshape(S, 1, H, HEAD_DIM))
        outs["vs"].append(ps[:, D_POOL + 2 * D_ATT:D_POOL + 3 * D_ATT].reshape(S, 1, H, HEAD_DIM))
        outs["ps"].append(jnp.concatenate([state_pool[l][:, 1:], ps[:, None, :D_POOL]], axis=1))
        outs["cs"].append(cv_s.reshape(S, 1, N_HEADS_CHUNK, HEAD_DIM))

        mo, mo_s = _dense(mix, mix_s, wb_out, out_dtype=BF16)
        xp, h2p = _ln_mod(xp, ln1_g[l], ln1_b[l], y=mo.reshape(B, T, D), mod_res=mod_p[l],
                          gate_col=2, mod=mod_p[l], sc_col=4)
        xs, h2s = _ln_mod(xs, ln1_g[l], ln1_b[l], y=mo_s.reshape(1, S, D), mod_res=mod_s[l],
                          gate_col=2, mod=mod_s[l], sc_col=4, h_dtype=F32)

        nxt = l + 1 < DEPTH
        p2 = p.reshape(B * T, D_IN)
        kv_rows = 2 * B * T // ((D_FF // 1024) * (B * T // 1024))
        a, a_s, wb_ff2, k_st, v_st = _dense(
            h2p.reshape(B * T, D), h2s.reshape(S, D), wb_ff1, out_dtype=BF16, relu2=True,
            casts=[(w_ff2, l, 128, D), (p2, K_COL1024, l, kv_rows, k_st), (p2, V_COL1024, l, kv_rows, v_st)])
        if nxt:
            f, wb_in, wb_out = _matmul(a, wb_ff2, tm=1024, tn=1024, tk=D,
                                       casts=[(w_in, l + 1, 32, D_IN), (w_out, l + 1, 32, D)])
        else:
            f = _matmul(a, wb_ff2, tm=1024, tn=1024, tk=D)
        f = f.reshape(B, T, D)
        f_s = _matmul(a_s, wb_ff2, tm=S, tn=1024, tk=D).reshape(1, S, D)
        xp, hp = _ln_mod(xp, ln2_g[l], ln2_b[l], y=f, mod_res=mod_p[l], gate_col=5,
                         mod=mod_p[l + 1] if nxt else None, sc_col=1)
        xs, hs = _ln_mod(xs, ln2_g[l], ln2_b[l], y=f_s, mod_res=mod_s[l], gate_col=5,
                         mod=mod_s[l + 1] if nxt else None, sc_col=1, h_dtype=F32)

    st = {k: jnp.stack(v) for k, v in outs.items()}
    k_st = k_st.reshape(DEPTH, B, T, H, HEAD_DIM)
    v_st = v_st.reshape(DEPTH, B, T, H, HEAD_DIM)
    return (xp, xs.reshape(S, 1, D), k_st, v_st, st["pp"], cv_st,
            st["ks"], st["vs"], st["ps"], st["cs"])
```
